```python
import math
import jax
import jax.numpy as jnp
from jax import lax
import numpy as np

D_MODEL = 1024
BATCH = 2
SEQ = 16384
DEPTH = 1

HEAD_DIM = 64
NSA_HEADS = 8
NSA_KV_HEADS = 2
NSA_REP = NSA_HEADS // NSA_KV_HEADS
CMP_LEN = 32
CMP_STRIDE = 16
CMP_HID = 256
SEL_LEN = 64
SEL_TOPN = 16
WINDOW = 512
SEL_FORCE = 1e4

MLA_HEADS = 8
Q_LORA = 256
KV_LORA = 128
QK_NOPE = 64
QK_ROPE = 32
V_DIM = 64
ROPE_THETA = 10000.0

N_BUCKETS = 32
MAX_DISTANCE = 2048

PEER_HEADS = 8
PEER_NKEYS = 128
PEER_EXPERTS = PEER_NKEYS * PEER_NKEYS
PEER_QDIM = 256
PEER_TOPK = 16
PEER_CHUNK = 128

Q_BLOCK = 128
EPS = 1e-6
NEG = -1e30

NSA_WIDTH = NSA_HEADS * HEAD_DIM
MLA_WIDTH = MLA_HEADS * V_DIM
KV_WIDTH = NSA_KV_HEADS * HEAD_DIM
IN_SIZES = (NSA_WIDTH,) + (KV_WIDTH,) * 6 + (3 * NSA_HEADS, Q_LORA, KV_LORA, QK_ROPE)
IN_TOTAL = NSA_WIDTH + 6 * KV_WIDTH + 3 * NSA_HEADS + Q_LORA + KV_LORA + QK_ROPE

kernel_name = "hymba_nsa_mla_peer_block"


def rmsnorm(x, g):
    xf = x.astype(jnp.float32)
    y = xf * lax.rsqrt(jnp.mean(xf * xf, axis=-1, keepdims=True) + EPS)
    return (y * g.astype(jnp.float32)).astype(x.dtype)


def split_columns(z):
    out, start = [], 0
    for size in IN_SIZES:
        out.append(z[..., start:start + size])
        start += size
    return out


def t5_bucket(dist):
    dist = jnp.maximum(dist, 0)
    max_exact = N_BUCKETS // 2
    d = jnp.maximum(dist, 1).astype(jnp.float32)
    large = max_exact + (jnp.log(d / max_exact) / math.log(MAX_DISTANCE / max_exact)
                         * (N_BUCKETS - max_exact)).astype(jnp.int32)
    return jnp.where(dist < max_exact, dist, jnp.minimum(large, N_BUCKETS - 1))


def rope(x, positions):
    half = QK_ROPE // 2
    inv = ROPE_THETA ** (-jnp.arange(half, dtype=jnp.float32) / half)
    ang = positions.astype(jnp.float32)[..., None] * inv
    if x.ndim == 4:
        ang = ang[:, :, None, :]
    cos, sin = jnp.cos(ang), jnp.sin(ang)
    xf = x.astype(jnp.float32)
    x1, x2 = xf[..., :half], xf[..., half:]
    return jnp.concatenate([x1 * cos - x2 * sin, x1 * sin + x2 * cos], axis=-1).astype(x.dtype)


def nsa_mixer(q, kc, vc, ks, vs, kw, vw, gates, cmp_pos, cmp_k_w1, cmp_k_w2, cmp_v_w1, cmp_v_w2, t5_bias):
    B, S = q.shape[0], q.shape[1]
    G, R, dh = NSA_KV_HEADS, NSA_REP, HEAD_DIM
    n_cmp = (S - CMP_LEN) // CMP_STRIDE + 1
    n_sel = S // SEL_LEN
    n_top = min(SEL_TOPN, n_sel)
    n_qb = S // Q_BLOCK
    wk = WINDOW + Q_BLOCK
    scale = dh ** -0.5

    cmp_start = jnp.arange(n_cmp) * CMP_STRIDE
    cmp_end = cmp_start + CMP_LEN - 1
    cidx = cmp_start[:, None] + jnp.arange(CMP_LEN)[None, :]

    def compress(raw, w1, w2):
        blk = raw[:, cidx] + cmp_pos[None, None, :, None, :]
        blk = jnp.moveaxis(blk, 3, 2).reshape(B, n_cmp, G, CMP_LEN * dh)
        return jax.nn.gelu(blk @ w1, approximate=False) @ w2

    k_cmp = compress(kc, cmp_k_w1, cmp_k_w2)
    v_cmp = compress(vc, cmp_v_w1, cmp_v_w2)

    sel_start = jnp.arange(n_sel) * SEL_LEN
    overlap = ((cmp_start[:, None] < sel_start[None, :] + SEL_LEN)
               & (cmp_end[:, None] >= sel_start[None, :])).astype(jnp.float32)
    ks_blk = ks.reshape(B, n_sel, SEL_LEN, G, dh).transpose(0, 3, 1, 2, 4)
    vs_blk = vs.reshape(B, n_sel, SEL_LEN, G, dh).transpose(0, 3, 1, 2, 4)
    gather_blocks = jax.vmap(jax.vmap(lambda blocks, idx: blocks[idx]))

    kw_pad = jnp.pad(kw, ((0, 0), (WINDOW, 0), (0, 0), (0, 0)))
    vw_pad = jnp.pad(vw, ((0, 0), (WINDOW, 0), (0, 0), (0, 0)))
    w_dist = WINDOW + jnp.arange(Q_BLOCK)[:, None] - jnp.arange(wk)[None, :]
    w_rel_ok = (w_dist >= 0) & (w_dist < WINDOW)
    w_bias = jnp.moveaxis(t5_bias[t5_bucket(w_dist)], -1, 0).reshape(G, R, Q_BLOCK, wk).astype(jnp.float32)

    bias_tbl = t5_bias.reshape(N_BUCKETS, G, R).astype(jnp.float32)
    g_ix = jnp.arange(G)[None, :, None, None]

    q_b = q.reshape(B, n_qb, Q_BLOCK, G, R, dh).transpose(1, 0, 3, 4, 2, 5)
    g_b = gates.reshape(B, n_qb, Q_BLOCK, G, R, 3).transpose(1, 0, 3, 4, 2, 5)

    def block(args):
        i, qi, gi = args
        t = i * Q_BLOCK + jnp.arange(Q_BLOCK)
        dist_c = t[:, None] - cmp_end[None, :]
        ok_c = dist_c >= 0
        bias_c = jnp.moveaxis(t5_bias[t5_bucket(dist_c)], -1, 0).reshape(G, R, Q_BLOCK, n_cmp).astype(jnp.float32)
        s_c = jnp.einsum('bgrqd,bkgd->bgrqk', qi, k_cmp).astype(jnp.float32) * scale + bias_c
        p_c = jax.nn.softmax(jnp.where(ok_c, s_c, NEG), axis=-1) * jnp.any(ok_c, axis=-1)[:, None]
        o_c = jnp.einsum('bgrqk,bkgd->bgrqd', p_c.astype(qi.dtype), v_cmp)
        imp = jnp.einsum('bgrqk,ks->bgqs', p_c, overlap)
        cur = t // SEL_LEN
        sid = jnp.arange(n_sel)[None, :]
        forced = (sid == 0) | (sid == cur[:, None]) | (sid == cur[:, None] - 1)
        imp = jnp.where(sid <= cur[:, None], imp + jnp.where(forced, SEL_FORCE, 0.0), NEG)
        _, sel = lax.top_k(imp, n_top)
        k_g = gather_blocks(ks_blk, sel).reshape(B, G, Q_BLOCK, n_top * SEL_LEN, dh)
        v_g = gather_blocks(vs_blk, sel).reshape(B, G, Q_BLOCK, n_top * SEL_LEN, dh)
        pos_g = (sel[..., None] * SEL_LEN + jnp.arange(SEL_LEN)).reshape(B, G, Q_BLOCK, n_top * SEL_LEN)
        dist_s = t[:, None] - pos_g
        bias_s = jnp.moveaxis(bias_tbl[t5_bucket(dist_s), g_ix], -1, 2)
        s_s = jnp.einsum('bgrqd,bgqkd->bgrqk', qi, k_g).astype(jnp.float32) * scale + bias_s
        p_s = jax.nn.softmax(jnp.where((dist_s >= 0)[:, :, None], s_s, NEG), axis=-1)
        o_s = jnp.einsum('bgrqk,bgqkd->bgrqd', p_s.astype(qi.dtype), v_g)
        k_w = lax.dynamic_slice_in_dim(kw_pad, i * Q_BLOCK, wk, axis=1)
        v_w = lax.dynamic_slice_in_dim(vw_pad, i * Q_BLOCK, wk, axis=1)
        ok_w = w_rel_ok & ((i * Q_BLOCK - WINDOW + jnp.arange(wk)) >= 0)[None, :]
        s_w = jnp.einsum('bgrqd,bkgd->bgrqk', qi, k_w).astype(jnp.float32) * scale + w_bias
        p_w = jax.nn.softmax(jnp.where(ok_w, s_w, NEG), axis=-1)
        o_w = jnp.einsum('bgrqk,bkgd->bgrqd', p_w.astype(qi.dtype), v_w)
        return gi[..., 0:1] * o_c + gi[..., 1:2] * o_s + gi[..., 2:3] * o_w

    out = lax.map(block, (jnp.arange(n_qb), q_b, g_b))
    return out.transpose(1, 0, 4, 2, 3, 5).reshape(B, S, NSA_WIDTH)


def mla_mixer(c_q, c_kv, k_rope_raw, positions, q_norm_g, w_q_up, kv_norm_g, w_kv_up):
    B, S = c_q.shape[0], c_q.shape[1]
    H = MLA_HEADS
    qh = (rmsnorm(c_q, q_norm_g) @ w_q_up).reshape(B, S, H, QK_NOPE + QK_ROPE)
    q_nope, q_rope = qh[..., :QK_NOPE], rope(qh[..., QK_NOPE:], positions)
    kv = (rmsnorm(c_kv, kv_norm_g) @ w_kv_up).reshape(B, S, H, QK_NOPE + V_DIM)
    k_nope, v = kv[..., :QK_NOPE], kv[..., QK_NOPE:]
    k_rope = rope(k_rope_raw, positions)
    scale = (QK_NOPE + QK_ROPE) ** -0.5
    n_qb = S // Q_BLOCK
    qn_b = q_nope.reshape(B, n_qb, Q_BLOCK, H, QK_NOPE).transpose(1, 0, 2, 3, 4)
    qr_b = q_rope.reshape(B, n_qb, Q_BLOCK, H, QK_ROPE).transpose(1, 0, 2, 3, 4)
    kpos = jnp.arange(S)

    def block(args):
        i, qn, qr = args
        s = (jnp.einsum('bqhd,bkhd->bhqk', qn, k_nope)
             + jnp.einsum('bqhd,bkd->bhqk', qr, k_rope)).astype(jnp.float32) * scale
        t = i * Q_BLOCK + jnp.arange(Q_BLOCK)
        p = jax.nn.softmax(jnp.where(kpos[None, :] <= t[:, None], s, NEG), axis=-1)
        return jnp.einsum('bhqk,bkhd->bqhd', p.astype(v.dtype), v)

    out = lax.map(block, (jnp.arange(n_qb), qn_b, qr_b))
    return out.transpose(1, 0, 2, 3, 4).reshape(B, S, MLA_WIDTH)


def peer_ffn(h, w_q, keys1, keys2, u, v):
    B, S, D = h.shape
    K = PEER_TOPK
    q = (h @ w_q).reshape(B, S, PEER_HEADS, 2, PEER_QDIM // 2)
    s1 = jnp.einsum('bshd,nd->bshn', q[..., 0, :], keys1).astype(jnp.float32)
    s2 = jnp.einsum('bshd,nd->bshn', q[..., 1, :], keys2).astype(jnp.float32)
    v1, i1 = lax.top_k(s1, K)
    v2, i2 = lax.top_k(s2, K)
    cand = (v1[..., :, None] + v2[..., None, :]).reshape(B, S, PEER_HEADS, K * K)
    cid = (i1[..., :, None] * PEER_NKEYS + i2[..., None, :]).reshape(B, S, PEER_HEADS, K * K)
    top, pos = lax.top_k(cand, K)
    eid = jnp.take_along_axis(cid, pos, axis=-1)
    gate = jax.nn.softmax(top, axis=-1).astype(h.dtype)
    n_ch = (B * S) // PEER_CHUNK
    h_c = h.reshape(n_ch, PEER_CHUNK, D)
    e_c = eid.reshape(n_ch, PEER_CHUNK, PEER_HEADS * K)
    g_c = gate.reshape(n_ch, PEER_CHUNK, PEER_HEADS * K)

    def chunk(args):
        hx, e, g = args
        act = jax.nn.gelu(jnp.einsum('cd,ced->ce', hx, u[e]), approximate=False)
        return jnp.einsum('ce,ced->cd', g * act, v[e])

    return lax.map(chunk, (h_c, e_c, g_c)).reshape(B, S, D)


def setup_inputs(seed: int = 0) -> dict:
    key = jax.random.key(seed)
    k = jax.random.split(key, 26)
    L = DEPTH

    def nrm(kk, shape, scale):
        return jax.random.normal(kk, shape, jnp.float32) * scale

    def gain(kk, shape):
        return 1.0 + 0.05 * jax.random.normal(kk, shape, jnp.float32)

    positions = (jnp.arange(SEQ, dtype=jnp.int32)[None, :]
                 + jax.random.randint(k[1], (BATCH, 1), 0, 4096, dtype=jnp.int32))
    return {
        "x": nrm(k[0], (BATCH, SEQ, D_MODEL), 1.0),
        "positions": positions,
        "norm1_g": gain(k[2], (L, D_MODEL)),
        "w_in": nrm(k[3], (L, D_MODEL, IN_TOTAL), D_MODEL ** -0.5),
        "cmp_pos": nrm(k[4], (L, CMP_LEN, HEAD_DIM), 0.1),
        "cmp_k_w1": nrm(k[5], (L, CMP_LEN * HEAD_DIM, CMP_HID), (CMP_LEN * HEAD_DIM) ** -0.5),
        "cmp_k_w2": nrm(k[6], (L, CMP_HID, HEAD_DIM), CMP_HID ** -0.5),
        "cmp_v_w1": nrm(k[7], (L, CMP_LEN * HEAD_DIM, CMP_HID), (CMP_LEN * HEAD_DIM) ** -0.5),
        "cmp_v_w2": nrm(k[8], (L, CMP_HID, HEAD_DIM), CMP_HID ** -0.5),
        "t5_bias": nrm(k[9], (N_BUCKETS, NSA_HEADS), 0.5),
        "q_norm_g": gain(k[10], (L, Q_LORA)),
        "w_q_up": nrm(k[11], (L, Q_LORA, MLA_HEADS * (QK_NOPE + QK_ROPE)), Q_LORA ** -0.5),
        "kv_norm_g": gain(k[12], (L, KV_LORA)),
        "w_kv_up": nrm(k[13], (L, KV_LORA, MLA_HEADS * (QK_NOPE + V_DIM)), KV_LORA ** -0.5),
        "grp_norm_nsa": gain(k[14], (L, NSA_WIDTH)),
        "grp_norm_mla": gain(k[15], (L, MLA_WIDTH)),
        "w_out": nrm(k[16], (L, NSA_WIDTH + MLA_WIDTH, D_MODEL), (NSA_WIDTH + MLA_WIDTH) ** -0.5),
        "norm2_g": gain(k[17], (L, D_MODEL)),
        "peer_wq": nrm(k[18], (L, D_MODEL, PEER_HEADS * PEER_QDIM), D_MODEL ** -0.5),
        "peer_keys1": nrm(k[19], (L, PEER_NKEYS, PEER_QDIM // 2), (PEER_QDIM // 2) ** -0.5),
        "peer_keys2": nrm(k[20], (L, PEER_NKEYS, PEER_QDIM // 2), (PEER_QDIM // 2) ** -0.5),
        "peer_u": nrm(k[21], (L, PEER_EXPERTS, D_MODEL), D_MODEL ** -0.5),
        "peer_v": nrm(k[22], (L, PEER_EXPERTS, D_MODEL), PEER_HEADS ** -0.5),
        "final_g": gain(k[23], (D_MODEL,)),
    }


def reference(x, positions, norm1_g, w_in, cmp_pos, cmp_k_w1, cmp_k_w2, cmp_v_w1, cmp_v_w2, t5_bias,
              q_norm_g, w_q_up, kv_norm_g, w_kv_up, grp_norm_nsa, grp_norm_mla, w_out, norm2_g,
              peer_wq, peer_keys1, peer_keys2, peer_u, peer_v, final_g):
    B, S, _ = x.shape
    for l in range(DEPTH):
        h = rmsnorm(x, norm1_g[l])
        cols = split_columns(h @ w_in[l])
        q, kc, vc, ks, vs, kw, vw = [c.reshape(B, S, -1, HEAD_DIM) for c in cols[:7]]
        gates = jax.nn.sigmoid(cols[7].astype(jnp.float32)).astype(x.dtype).reshape(B, S, NSA_HEADS, 3)
        c_q, c_kv, k_rope = cols[8], cols[9], cols[10]
        o_nsa = nsa_mixer(q, kc, vc, ks, vs, kw, vw, gates, cmp_pos[l], cmp_k_w1[l], cmp_k_w2[l],
                          cmp_v_w1[l], cmp_v_w2[l], t5_bias)
        o_mla = mla_mixer(c_q, c_kv, k_rope, positions, q_norm_g[l], w_q_up[l], kv_norm_g[l], w_kv_up[l])
        mixed = jnp.concatenate([rmsnorm(o_nsa, grp_norm_nsa[l]), rmsnorm(o_mla, grp_norm_mla[l])], axis=-1)
        x = x + mixed @ w_out[l]
        x = x + peer_ffn(rmsnorm(x, norm2_g[l]), peer_wq[l], peer_keys1[l], peer_keys2[l], peer_u[l], peer_v[l])
    return rmsnorm(x, final_g)
```

```python
import functools
import math

import numpy as np
import jax
import jax.numpy as jnp
from jax import lax
from jax.experimental import pallas as pl
from jax.experimental.pallas import tpu as pltpu

F32 = jnp.float32
BF16 = jnp.bfloat16

HEAD_DIM = 64
NSA_HEADS = 8
NSA_KV_HEADS = 2
NSA_REP = NSA_HEADS // NSA_KV_HEADS
CMP_LEN = 32
CMP_STRIDE = 16
SEL_LEN = 64
SEL_TOPN = 16
WINDOW = 512
SEL_FORCE = 1e4
MLA_HEADS = 8
QK_NOPE = 64
QK_ROPE = 32
V_DIM = 64
ROPE_THETA = 10000.0
N_BUCKETS = 32
MAX_DISTANCE = 2048
PEER_HEADS = 8
PEER_TOPK = 16
EPS = 1e-6
NEG = -1e30
REMOVED = -3e38

LANE = 128
NSA_W = NSA_HEADS * HEAD_DIM
KV_W = NSA_KV_HEADS * HEAD_DIM
MLA_QK = 128
FAR_DIST = 2048

TM_IN = 512
TQ_CMP = 128
TQ = 256
TK = 512
TM_OUT = 256
TM_TOPK = 256
TM_PEER = 512
NE_PEER = 512
SEL_EMAX = (FAR_DIST + TK - 1) // TQ
SEL_ROFF = TQ * SEL_EMAX + LANE
WIN_ROWS = WINDOW + TQ
VMEM_LIMIT = 56 * 1024 * 1024


def _cparams(*sem):
    return pltpu.CompilerParams(dimension_semantics=sem, vmem_limit_bytes=VMEM_LIMIT)


def _rms(x):
    return x * lax.rsqrt(jnp.mean(x * x, axis=-1, keepdims=True) + EPS)


def _dot(a, b):
    return jnp.dot(a, b, preferred_element_type=F32)


def _inproj_kernel(x_ref, g1_ref, win_ref, qng_ref, kvng_ref, wqa_ref, wqb_ref, wk_ref, wv_ref,
                   prope_ref, cq_ref, sq_ref, ck_ref,
                   qT_ref, kc_ref, vc_ref, ks_ref, kw_ref, vsT_ref, vwT_ref, gT_ref,
                   qmT_ref, km_ref, vmT_ref):
    x = x_ref[0]
    h = _rms(x) * g1_ref[...]
    z = _dot(h.astype(BF16), win_ref[...])
    qT_ref[0] = (z[:, :NSA_W] * (HEAD_DIM ** -0.5)).T.astype(BF16)
    o = NSA_W
    kc_ref[0] = z[:, o:o + KV_W]
    vc_ref[0] = z[:, o + KV_W:o + 2 * KV_W]
    ks_ref[0] = z[:, o + 2 * KV_W:o + 3 * KV_W].astype(BF16)
    vsT_ref[0] = z[:, o + 3 * KV_W:o + 4 * KV_W].T.astype(BF16)
    kw_ref[0] = z[:, o + 4 * KV_W:o + 5 * KV_W].astype(BF16)
    vwT_ref[0] = z[:, o + 5 * KV_W:o + 6 * KV_W].T.astype(BF16)
    o = o + 6 * KV_W
    cq = z[:, o:o + 256]
    ckv = z[:, o + 256:o + 384]
    misc = z[:, o + 384:o + 512]
    gT_ref[0] = jax.nn.sigmoid(misc).T
    cqn = (_rms(cq) * qng_ref[...]).astype(BF16)
    cos_q = jnp.tile(cq_ref[0], (1, MLA_HEADS))
    sin_q = jnp.tile(sq_ref[0], (1, MLA_HEADS))
    qm = _dot(cqn, wqa_ref[...]) * cos_q + _dot(cqn, wqb_ref[...]) * sin_q
    qmT_ref[0] = qm.T.astype(BF16)
    ckvn = (_rms(ckv) * kvng_ref[...]).astype(BF16)
    kr = (misc * ck_ref[0]).astype(BF16)
    km_ref[0] = (_dot(ckvn, wk_ref[...]) + _dot(kr, prope_ref[...])).astype(BF16)
    vmT_ref[0] = _dot(ckvn, wv_ref[...]).T.astype(BF16)


def _inproj(x, g1, win_p, qng, kvng, wqa, wqb, wk, wv, prope, cq_tab, sq_tab, ck_tab):
    B, S, D = x.shape
    tm = TM_IN
    full = lambda a: pl.BlockSpec(a.shape, lambda b, i: (0,) * a.ndim)
    tok = lambda w: pl.BlockSpec((1, tm, w), lambda b, i: (b, i, 0))
    tr = lambda w: pl.BlockSpec((1, w, tm), lambda b, i: (b, 0, i))
    outs = [
        (jax.ShapeDtypeStruct((B, NSA_W, S), BF16), tr(NSA_W)),
        (jax.ShapeDtypeStruct((B, S, KV_W), F32), tok(KV_W)),
        (jax.ShapeDtypeStruct((B, S, KV_W), F32), tok(KV_W)),
        (jax.ShapeDtypeStruct((B, S, KV_W), BF16), tok(KV_W)),
        (jax.ShapeDtypeStruct((B, S, KV_W), BF16), tok(KV_W)),
        (jax.ShapeDtypeStruct((B, KV_W, S), BF16), tr(KV_W)),
        (jax.ShapeDtypeStruct((B, KV_W, S), BF16), tr(KV_W)),
        (jax.ShapeDtypeStruct((B, LANE, S), F32), tr(LANE)),
        (jax.ShapeDtypeStruct((B, MLA_HEADS * MLA_QK, S), BF16), tr(MLA_HEADS * MLA_QK)),
        (jax.ShapeDtypeStruct((B, S, MLA_HEADS * MLA_QK), BF16), tok(MLA_HEADS * MLA_QK)),
        (jax.ShapeDtypeStruct((B, MLA_HEADS * V_DIM, S), BF16), tr(MLA_HEADS * V_DIM)),
    ]
    return pl.pallas_call(
        _inproj_kernel,
        grid=(B, S // tm),
        in_specs=[tok(D), full(g1), full(win_p), full(qng), full(kvng), full(wqa), full(wqb),
                  full(wk), full(wv), full(prope), tok(LANE), tok(LANE), tok(LANE)],
        out_specs=[o[1] for o in outs],
        out_shape=[o[0] for o in outs],
        compiler_params=_cparams("parallel", "parallel"),
        name="inproj",
    )(x, g1, win_p, qng, kvng, wqa, wqb, wk, wv, prope, cq_tab, sq_tab, ck_tab)


def _gelu(x):
    return 0.5 * x * (1.0 + lax.erf(x * (1.0 / math.sqrt(2.0))))


def _compress_kernel(c_ref, plo_ref, phi_ref, w1lo_ref, w1hi_ref, w2_ref, out_ref):
    c = c_ref[0, 0]
    nc = c.shape[0]
    a = _dot((c + plo_ref[...]).astype(BF16), w1lo_ref[0])
    b = _dot((c + phi_ref[...]).astype(BF16), w1hi_ref[0])
    hid = a + pltpu.roll(b, nc - 1, 0)
    out_ref[0, 0] = _dot(_gelu(hid).astype(BF16), w2_ref[0]).astype(BF16)


def _compress(c_all, pos_lo, pos_hi, w1lo, w1hi, w2):
    _, BG, nc, W = c_all.shape
    return pl.pallas_call(
        _compress_kernel,
        grid=(2, BG),
        in_specs=[pl.BlockSpec((1, 1, nc, W), lambda t, i: (t, i, 0, 0)),
                  pl.BlockSpec(pos_lo.shape, lambda t, i: (0, 0)),
                  pl.BlockSpec(pos_hi.shape, lambda t, i: (0, 0)),
                  pl.BlockSpec((1,) + w1lo.shape[1:], lambda t, i: (t, 0, 0)),
                  pl.BlockSpec((1,) + w1hi.shape[1:], lambda t, i: (t, 0, 0)),
                  pl.BlockSpec((1,) + w2.shape[1:], lambda t, i: (t, 0, 0))],
        out_specs=pl.BlockSpec((1, 1, nc, HEAD_DIM), lambda t, i: (t, i, 0, 0)),
        out_shape=jax.ShapeDtypeStruct((2, BG, nc, HEAD_DIM), BF16),
        compiler_params=_cparams("parallel", "parallel"),
        name="compress",
    )(c_all, pos_lo, pos_hi, w1lo, w1hi, w2)


def _nsa_cmp_kernel(qT_ref, kc_ref, vcT_ref, ft_ref, ovT_ref, ocT_ref, selb_ref):
    qi = pl.program_id(2)
    tq = qT_ref.shape[2]
    nc = kc_ref.shape[1]
    nsel = ovT_ref.shape[0]
    k = kc_ref[0]
    vT = vcT_ref[0]
    start = pl.multiple_of(nc - (tq // CMP_STRIDE) * qi, 8)
    psum = jnp.zeros((nc, tq), F32)
    for r in range(NSA_REP):
        s = _dot(k, qT_ref[0, r * HEAD_DIM:(r + 1) * HEAD_DIM, :])
        s = s + ft_ref[r, pl.ds(start, nc), :]
        m = jnp.max(s, axis=0, keepdims=True)
        p = jnp.exp(s - m)
        l = jnp.sum(p, axis=0, keepdims=True)
        p = p * jnp.where(m > 0.5 * NEG, 1.0 / l, 0.0)
        ocT_ref[0, r * HEAD_DIM:(r + 1) * HEAD_DIM, :] = _dot(vT, p.astype(BF16))
        psum = psum + p
    p_hi = psum.astype(BF16)
    p_lo = (psum - p_hi.astype(F32)).astype(BF16)
    imp = _dot(ovT_ref[...], p_hi) + _dot(ovT_ref[...], p_lo)

    sid = lax.broadcasted_iota(jnp.int32, (nsel, tq), 0)
    t = qi * tq + lax.broadcasted_iota(jnp.int32, (nsel, tq), 1)
    cur = jnp.right_shift(t, int(math.log2(SEL_LEN)))
    valid = sid <= cur
    forced = (sid == 0) | (sid == cur) | (sid == cur - 1)
    n_free = SEL_TOPN - 1 - (cur[0:1] >= 1).astype(jnp.int32) - (cur[0:1] >= 2).astype(jnp.int32)
    work = jnp.where(valid & jnp.logical_not(forced), imp, NEG)
    tau = jnp.zeros((1, tq), F32)
    for it in range(1, SEL_TOPN):
        mx = jnp.max(work, axis=0, keepdims=True)
        if it >= SEL_TOPN - 3:
            tau = jnp.where(n_free == it, mx, tau)
        if it < SEL_TOPN - 1:
            work = jnp.where(work >= mx, REMOVED, work)
    sel = valid & (forced | (imp >= tau))
    selb_ref[0, 0] = jnp.where(sel, 0.0, NEG)


def _nsa_cmp(qT, kcmp, vcmpT, ft, ovT):
    B, _, S = qT.shape
    G = NSA_KV_HEADS
    nc = kcmp.shape[1]
    nsel = ovT.shape[0]
    tq = TQ_CMP
    gw = NSA_REP * HEAD_DIM
    return pl.pallas_call(
        _nsa_cmp_kernel,
        grid=(B, G, S // tq),
        in_specs=[pl.BlockSpec((1, gw, tq), lambda b, g, i: (b, g, i)),
                  pl.BlockSpec((1, nc, HEAD_DIM), lambda b, g, i: (b * G + g, 0, 0)),
                  pl.BlockSpec((1, HEAD_DIM, nc), lambda b, g, i: (b * G + g, 0, 0)),
                  pl.BlockSpec((NSA_REP, 2 * nc, tq), lambda b, g, i: (g, 0, 0)),
                  pl.BlockSpec(ovT.shape, lambda b, g, i: (0, 0))],
        out_specs=[pl.BlockSpec((1, gw, tq), lambda b, g, i: (b, g, i)),
                   pl.BlockSpec((1, 1, nsel, tq), lambda b, g, i: (b, g, 0, i))],
        out_shape=[jax.ShapeDtypeStruct((B, NSA_W, S), F32),
                   jax.ShapeDtypeStruct((B, G, nsel, S), F32)],
        compiler_params=_cparams("parallel", "parallel", "parallel"),
        name="nsa_cmp",
    )(qT, kcmp, vcmpT, ft, ovT)


def _online_update(s, vT, m_ref, l_ref, acc_ref, idx):
    m_old = m_ref[idx]
    m_new = jnp.maximum(m_old, jnp.max(s, axis=0, keepdims=True))
    alpha = jnp.exp(m_old - m_new)
    p = jnp.exp(s - m_new)
    l_ref[idx] = alpha * l_ref[idx] + jnp.sum(p, axis=0, keepdims=True)
    acc_ref[idx] = alpha * acc_ref[idx] + _dot(vT, p.astype(BF16))
    m_ref[idx] = m_new


def _nsa_sel_kernel(qT_ref, ks_ref, vsT_ref, selb_ref, ts_ref, osT_ref, m_ref, l_ref, acc_ref):
    qi = pl.program_id(2)
    tq = qT_ref.shape[2]
    m_ref[...] = jnp.full(m_ref.shape, REMOVED, F32)
    l_ref[...] = jnp.zeros(l_ref.shape, F32)
    acc_ref[...] = jnp.zeros(acc_ref.shape, F32)
    blocks = TK // SEL_LEN

    def chunk(c, near):
        k = ks_ref[0, 0, pl.ds(pl.multiple_of(c * TK, TK), TK), :]
        vT = vsT_ref[0, 0, c]
        mb = selb_ref[0, 0, pl.ds(pl.multiple_of(c * blocks, blocks), blocks), :]
        mbias = jnp.broadcast_to(mb[:, None, :], (blocks, SEL_LEN, tq)).reshape(TK, tq)
        if near:
            e = qi - c * (TK // TQ)
            base = SEL_ROFF - TQ * e
        for r in range(NSA_REP):
            s = _dot(k, qT_ref[0, r * HEAD_DIM:(r + 1) * HEAD_DIM, :]) + mbias
            if near:
                halves = [ts_ref[r, pl.ds(pl.multiple_of(base - LANE * a, LANE), TK), :]
                          for a in range(tq // LANE)]
                s = s + jnp.concatenate(halves, axis=1)
            _online_update(s, vT, m_ref, l_ref, acc_ref, r)

    n_all = (qi * TQ) // TK + 1
    n_far = jnp.maximum(0, (qi - SEL_EMAX + 1) // (TK // TQ))

    def far_body(c, carry):
        chunk(c, False)
        return carry

    def near_body(c, carry):
        chunk(c, True)
        return carry

    lax.fori_loop(0, n_far, far_body, 0)
    lax.fori_loop(n_far, n_all, near_body, 0)
    for r in range(NSA_REP):
        osT_ref[0, r * HEAD_DIM:(r + 1) * HEAD_DIM, :] = acc_ref[r] / l_ref[r]


def _nsa_sel(qT, ks4, vsT5, selb, ts):
    B, _, S = qT.shape
    G = NSA_KV_HEADS
    nsel = selb.shape[2]
    gw = NSA_REP * HEAD_DIM
    return pl.pallas_call(
        _nsa_sel_kernel,
        grid=(B, G, S // TQ),
        in_specs=[pl.BlockSpec((1, gw, TQ), lambda b, g, i: (b, g, i)),
                  pl.BlockSpec((1, 1, S, HEAD_DIM), lambda b, g, i: (b, g, 0, 0)),
                  pl.BlockSpec((1, 1, S // TK, HEAD_DIM, TK), lambda b, g, i: (b, g, 0, 0, 0)),
                  pl.BlockSpec((1, 1, nsel, TQ), lambda b, g, i: (b, g, 0, i)),
                  pl.BlockSpec((NSA_REP,) + ts.shape[1:], lambda b, g, i: (g, 0, 0))],
        out_specs=pl.BlockSpec((1, gw, TQ), lambda b, g, i: (b, g, i)),
        out_shape=jax.ShapeDtypeStruct((B, NSA_W, S), F32),
        scratch_shapes=[pltpu.VMEM((NSA_REP, 1, TQ), F32), pltpu.VMEM((NSA_REP, 1, TQ), F32),
                        pltpu.VMEM((NSA_REP, HEAD_DIM, TQ), F32)],
        compiler_params=_cparams("parallel", "parallel", "parallel"),
        name="nsa_sel",
    )(qT, ks4, vsT5, selb, ts)


def _nsa_win_kernel(qT_ref, kw_ref, vwT_ref, tw_ref, owT_ref):
    qi = pl.program_id(2)
    tq = qT_ref.shape[2]
    k = kw_ref[0, 0, pl.ds(pl.multiple_of(qi * TQ, TQ), WIN_ROWS), :]
    row = lax.broadcasted_iota(jnp.int32, (WIN_ROWS, tq), 0)
    in_seq = row >= WINDOW - qi * TQ
    for r in range(NSA_REP):
        s = _dot(k, qT_ref[0, r * HEAD_DIM:(r + 1) * HEAD_DIM, :])
        halves = [tw_ref[r, LANE * (tq // LANE - 1 - a):LANE * (tq // LANE - 1 - a) + WIN_ROWS, :]
                  for a in range(tq // LANE)]
        s = jnp.where(in_seq, s + jnp.concatenate(halves, axis=1), NEG)
        m = jnp.max(s, axis=0, keepdims=True)
        p = jnp.exp(s - m)
        l = jnp.sum(p, axis=0, keepdims=True)
        pb = p.astype(BF16)
        acc = jnp.zeros((HEAD_DIM, tq), F32)
        for j in range(WIN_ROWS // TQ):
            acc = acc + _dot(vwT_ref[0, 0, qi + j], pb[j * TQ:(j + 1) * TQ])
        owT_ref[0, r * HEAD_DIM:(r + 1) * HEAD_DIM, :] = acc / l


def _nsa_win(qT, kwp4, vwT5, tw):
    B, _, S = qT.shape
    G = NSA_KV_HEADS
    gw = NSA_REP * HEAD_DIM
    return pl.pallas_call(
        _nsa_win_kernel,
        grid=(B, G, S // TQ),
        in_specs=[pl.BlockSpec((1, gw, TQ), lambda b, g, i: (b, g, i)),
                  pl.BlockSpec((1, 1) + kwp4.shape[2:], lambda b, g, i: (b, g, 0, 0)),
                  pl.BlockSpec((1, 1) + vwT5.shape[2:], lambda b, g, i: (b, g, 0, 0, 0)),
                  pl.BlockSpec((NSA_REP,) + tw.shape[1:], lambda b, g, i: (g, 0, 0))],
        out_specs=pl.BlockSpec((1, gw, TQ), lambda b, g, i: (b, g, i)),
        out_shape=jax.ShapeDtypeStruct((B, NSA_W, S), F32),
        compiler_params=_cparams("parallel", "parallel", "parallel"),
        name="nsa_win",
    )(qT, kwp4, vwT5, tw)


def _mla_kernel(qT_ref, k_ref, vT_ref, oT_ref, m_ref, l_ref, acc_ref):
    qi = pl.program_id(2)
    tq = qT_ref.shape[2]
    m_ref[...] = jnp.full(m_ref.shape, REMOVED, F32)
    l_ref[...] = jnp.zeros(l_ref.shape, F32)
    acc_ref[...] = jnp.zeros(acc_ref.shape, F32)
    q = qT_ref[0]

    def chunk(c, masked):
        k = k_ref[0, pl.ds(pl.multiple_of(c * TK, TK), TK), :]
        s = _dot(k, q)
        if masked:
            kpos = c * TK + lax.broadcasted_iota(jnp.int32, (TK, tq), 0)
            qpos = qi * TQ + lax.broadcasted_iota(jnp.int32, (TK, tq), 1)
            s = jnp.where(kpos <= qpos, s, NEG)
        _online_update(s, vT_ref[0, 0, c], m_ref, l_ref, acc_ref, 0)

    n_full = (qi * TQ) // TK

    def body(c, carry):
        chunk(c, False)
        return carry

    lax.fori_loop(0, n_full, body, 0)
    chunk(n_full, True)
    oT_ref[0] = acc_ref[0] / l_ref[0]


def _mla(qmT, km, vmT5):
    B, _, S = qmT.shape
    H = MLA_HEADS
    return pl.pallas_call(
        _mla_kernel,
        grid=(B, H, S // TQ),
        in_specs=[pl.BlockSpec((1, MLA_QK, TQ), lambda b, h, i: (b, h, i)),
                  pl.BlockSpec((1, S, MLA_QK), lambda b, h, i: (b, 0, h)),
                  pl.BlockSpec((1, 1, S // TK, V_DIM, TK), lambda b, h, i: (b, h, 0, 0, 0))],
        out_specs=pl.BlockSpec((1, V_DIM, TQ), lambda b, h, i: (b, h, i)),
        out_shape=jax.ShapeDtypeStruct((B, H * V_DIM, S), F32),
        scratch_shapes=[pltpu.VMEM((1, 1, TQ), F32), pltpu.VMEM((1, 1, TQ), F32),
                        pltpu.VMEM((1, V_DIM, TQ), F32)],
        compiler_params=_cparams("parallel", "parallel", "parallel"),
        name="mla",
    )(qmT, km, vmT5)


def _outproj_kernel(ocT_ref, osT_ref, owT_ref, omT_ref, gT_ref, x_ref, gn_ref, gm_ref, wout_ref,
                    g2_ref, x2_ref, hnT_ref):
    parts = []
    for h in range(NSA_HEADS):
        rows = slice(h * HEAD_DIM, (h + 1) * HEAD_DIM)
        parts.append(gT_ref[0, 3 * h:3 * h + 1, :] * ocT_ref[0, rows, :]
                     + gT_ref[0, 3 * h + 1:3 * h + 2, :] * osT_ref[0, rows, :]
                     + gT_ref[0, 3 * h + 2:3 * h + 3, :] * owT_ref[0, rows, :])
    nsaT = jnp.concatenate(parts, axis=0)

    def norm_t(yT, g):
        y = yT * lax.rsqrt(jnp.mean(yT * yT, axis=0, keepdims=True) + EPS)
        return (y.T * g).astype(BF16)

    y_nsa = norm_t(nsaT, gn_ref[...])
    y_mla = norm_t(omT_ref[0], gm_ref[...])
    x2 = x_ref[0] + _dot(y_nsa, wout_ref[:NSA_W, :]) + _dot(y_mla, wout_ref[NSA_W:, :])
    x2_ref[0] = x2
    hnT_ref[...] = (_rms(x2) * g2_ref[...]).T.astype(BF16)


def _outproj(ocT, osT, owT, omT, gT, x, gn, gm, wout, g2):
    B, S, D = x.shape
    tm = TM_OUT
    nt = S // tm
    tr = lambda a: pl.BlockSpec((1, a.shape[1], tm), lambda b, i: (b, 0, i))
    full = lambda a: pl.BlockSpec(a.shape, lambda b, i: (0,) * a.ndim)
    return pl.pallas_call(
        _outproj_kernel,
        grid=(B, nt),
        in_specs=[tr(ocT), tr(osT), tr(owT), tr(omT), tr(gT),
                  pl.BlockSpec((1, tm, D), lambda b, i: (b, i, 0)),
                  full(gn), full(gm), full(wout), full(g2)],
        out_specs=[pl.BlockSpec((1, tm, D), lambda b, i: (b, i, 0)),
                   pl.BlockSpec((D, tm), lambda b, i: (0, b * nt + i))],
        out_shape=[jax.ShapeDtypeStruct((B, S, D), F32),
                   jax.ShapeDtypeStruct((D, B * S), BF16)],
        compiler_params=_cparams("parallel", "parallel"),
        name="outproj",
    )(ocT, osT, owT, omT, gT, x, gn, gm, wout, g2)


def _row_max_bcast(w3):
    m8 = jnp.max(w3, axis=0)
    for sh in (4, 2, 1):
        m8 = jnp.maximum(m8, pltpu.roll(m8, sh, 0))
    return m8


def _top_values(s, n):
    w3 = s.reshape(s.shape[0] // 8, 8, s.shape[1])
    vals = []
    for it in range(n):
        mx = _row_max_bcast(w3)
        vals.append(mx)
        if it < n - 1:
            w3 = jnp.where(w3 >= mx[None], REMOVED, w3)
    return vals


def _pair_list(n):
    return [(i, j) for i in range(n) for j in range(n) if (i + 1) * (j + 1) <= n]


def _kth_of_pairs(a, b, n, m_shift=None):
    cands = [a[i] + b[j] for i, j in _pair_list(n)]
    z = None
    mx = None
    for it in range(n):
        mx = functools.reduce(jnp.maximum, cands)
        if m_shift is not None:
            e = jnp.exp(mx - m_shift)
            z = e if z is None else z + e
        if it < n - 1:
            cands = [jnp.where(c >= mx, REMOVED, c) for c in cands]
    return mx, z


def _peer_topk_kernel(hnT_ref, wqT_ref, k1_ref, k2_ref, s1p_ref, s2_ref, thr_ref):
    tm = hnT_ref.shape[1]
    n = PEER_TOPK
    qT = _dot(wqT_ref[...], hnT_ref[...]).astype(BF16)
    dk = k1_ref.shape[1]
    sub = lax.broadcasted_iota(jnp.int32, (8, tm), 0)
    a_m = [jnp.zeros((8, tm), F32)] * n
    b_m = [jnp.zeros((8, tm), F32)] * n
    for h in range(PEER_HEADS):
        s1 = _dot(k1_ref[...], qT[2 * h * dk:(2 * h + 1) * dk])
        s2 = _dot(k2_ref[...], qT[(2 * h + 1) * dk:(2 * h + 2) * dk])
        s1p_ref[h] = s1
        s2_ref[h] = s2
        a = _top_values(s1, n)
        b = _top_values(s2, n)
        a_m = [jnp.where(sub == h, a[i], a_m[i]) for i in range(n)]
        b_m = [jnp.where(sub == h, b[i], b_m[i]) for i in range(n)]
    m_top = a_m[0] + b_m[0]
    _, z = _kth_of_pairs(a_m, b_m, n, m_top)
    shift = m_top + jnp.log(z)
    for h in range(PEER_HEADS):
        s1p_ref[h] = s1p_ref[h] - shift[h:h + 1, :]
    a_s = [a_m[i] - shift for i in range(n)]
    thr, _ = _kth_of_pairs(a_s, b_m, n)
    thr_ref[...] = thr


def _peer_topk(hnT, wqT, k1, k2):
    D, N = hnT.shape
    tm = TM_TOPK
    H = PEER_HEADS
    nk = k1.shape[0]
    full = lambda a: pl.BlockSpec(a.shape, lambda i: (0,) * a.ndim)
    return pl.pallas_call(
        _peer_topk_kernel,
        grid=(N // tm,),
        in_specs=[pl.BlockSpec((D, tm), lambda i: (0, i)), full(wqT), full(k1), full(k2)],
        out_specs=[pl.BlockSpec((H, nk, tm), lambda i: (0, 0, i)),
                   pl.BlockSpec((H, nk, tm), lambda i: (0, 0, i)),
                   pl.BlockSpec((H, tm), lambda i: (0, i))],
        out_shape=[jax.ShapeDtypeStruct((H, nk, N), F32),
                   jax.ShapeDtypeStruct((H, nk, N), F32),
                   jax.ShapeDtypeStruct((H, N), F32)],
        compiler_params=_cparams("parallel"),
        name="peer_topk",
    )(hnT, wqT, k1, k2)


def _peer_dense_kernel(hnT_ref, u_ref, vT_ref, s1p_ref, s2_ref, thr_ref, x2_ref, gf_ref, out_ref,
                       acc_ref):
    j = pl.program_id(1)
    nk = s2_ref.shape[1]
    ne = u_ref.shape[0]
    tm = hnT_ref.shape[1]

    @pl.when(j == 0)
    def _():
        acc_ref[...] = jnp.zeros(acc_ref.shape, F32)

    act = _dot(u_ref[...], hnT_ref[...])
    ws = []
    for kk in range(ne // nk):
        i1 = j * (ne // nk) + kk
        gate = jnp.zeros((nk, tm), F32)
        for h in range(PEER_HEADS):
            sc = s1p_ref[h, pl.ds(i1, 1), :] + s2_ref[h]
            gate = gate + jnp.where(sc >= thr_ref[h:h + 1, :], jnp.exp(sc), 0.0)
        ws.append((gate * _gelu(act[kk * nk:(kk + 1) * nk])).astype(BF16))
    acc_ref[...] += _dot(vT_ref[...], jnp.concatenate(ws, axis=0))

    @pl.when(j == pl.num_programs(1) - 1)
    def _():
        y = x2_ref[...] + acc_ref[...].T
        out_ref[...] = _rms(y) * gf_ref[...]


def _peer_dense(hnT, u, vT, s1p, s2, thr, x2, gf):
    D, N = hnT.shape
    NE = u.shape[0]
    tm, ne = TM_PEER, NE_PEER
    H, nk, _ = s2.shape
    return pl.pallas_call(
        _peer_dense_kernel,
        grid=(N // tm, NE // ne),
        in_specs=[pl.BlockSpec((D, tm), lambda i, j: (0, i)),
                  pl.BlockSpec((ne, D), lambda i, j: (j, 0)),
                  pl.BlockSpec((D, ne), lambda i, j: (0, j)),
                  pl.BlockSpec((H, nk, tm), lambda i, j: (0, 0, i)),
                  pl.BlockSpec((H, nk, tm), lambda i, j: (0, 0, i)),
                  pl.BlockSpec((H, tm), lambda i, j: (0, i)),
                  pl.BlockSpec((tm, D), lambda i, j: (i, 0)),
                  pl.BlockSpec(gf.shape, lambda i, j: (0, 0))],
        out_specs=pl.BlockSpec((tm, D), lambda i, j: (i, 0)),
        out_shape=jax.ShapeDtypeStruct((N, D), F32),
        scratch_shapes=[pltpu.VMEM((D, tm), F32)],
        compiler_params=_cparams("parallel", "arbitrary"),
        name="peer_dense",
    )(hnT, u, vT, s1p, s2, thr, x2, gf)


def _t5_bucket(dist):
    dist = jnp.maximum(dist, 0)
    max_exact = N_BUCKETS // 2
    d = jnp.maximum(dist, 1).astype(F32)
    large = max_exact + (jnp.log(d / max_exact) / math.log(MAX_DISTANCE / max_exact)
                         * (N_BUCKETS - max_exact)).astype(jnp.int32)
    return jnp.where(dist < max_exact, dist, jnp.minimum(large, N_BUCKETS - 1))


def _bias_tables(t5_bias, nc):
    far = t5_bias[N_BUCKETS - 1].astype(F32)

    def rel(d):
        return jnp.moveaxis(t5_bias[_t5_bucket(d)].astype(F32), -1, 0)

    i = jnp.arange(LANE)[None, :]
    m = jnp.arange(-nc, nc)[:, None]
    d = i - CMP_STRIDE * m - (CMP_LEN - 1)
    ft = jnp.where(d >= 0, rel(d), NEG)
    r = jnp.arange(SEL_ROFF + TK)[:, None]
    d = i - r + SEL_ROFF
    ts = jnp.where(d >= 0, rel(d) - far[:, None, None], NEG)
    r = jnp.arange(LANE * (TQ // LANE - 1) + WIN_ROWS)[:, None]
    d = i - r + LANE * (TQ // LANE - 1) + WINDOW
    tw = jnp.where((d >= 0) & (d < WINDOW), rel(d), NEG)
    return ft, ts, tw


def _overlap_t(nc, n_cmp, nsel):
    cs = np.arange(nc) * CMP_STRIDE
    ce = cs + CMP_LEN - 1
    ss = np.arange(nsel) * SEL_LEN
    ov = (cs[None, :] < ss[:, None] + SEL_LEN) & (ce[None, :] >= ss[:, None]) & (np.arange(nc)[None, :] < n_cmp)
    return jnp.asarray(ov, BF16)


def kernel(x, positions, norm1_g, w_in, cmp_pos, cmp_k_w1, cmp_k_w2, cmp_v_w1, cmp_v_w2, t5_bias,
           q_norm_g, w_q_up, kv_norm_g, w_kv_up, grp_norm_nsa, grp_norm_mla, w_out, norm2_g,
           peer_wq, peer_keys1, peer_keys2, peer_u, peer_v, final_g):
    B, S, D = x.shape
    G, H = NSA_KV_HEADS, MLA_HEADS
    assert S % TK == 0 and S % (CMP_STRIDE * 8) == 0 and norm1_g.shape[0] == 1
    nc = S // CMP_STRIDE
    n_cmp = (S - CMP_LEN) // CMP_STRIDE + 1
    nsel = S // SEL_LEN
    half = QK_ROPE // 2

    w = w_in[0]
    o_g = NSA_W + 6 * KV_W
    n_gate = 3 * NSA_HEADS
    o_cq = o_g + n_gate
    q_lora = w_q_up.shape[1]
    kv_lora = w_kv_up.shape[1]
    o_ckv = o_cq + q_lora
    o_kr = o_ckv + kv_lora
    kr1, kr2 = w[:, o_kr:o_kr + half], w[:, o_kr + half:o_kr + 2 * half]
    misc = jnp.concatenate([kr1, kr2, -kr2, kr1, w[:, o_g:o_g + n_gate],
                            jnp.zeros((D, LANE - 4 * half - n_gate), F32)], axis=1)
    win_p = jnp.concatenate([w[:, :o_g], w[:, o_cq:o_kr], misc], axis=1).astype(BF16)

    wq = w_q_up[0].reshape(q_lora, H, QK_NOPE + QK_ROPE)
    q1, q2 = wq[..., QK_NOPE:QK_NOPE + half], wq[..., QK_NOPE + half:]
    zpad = jnp.zeros((q_lora, H, MLA_QK - QK_NOPE - QK_ROPE), F32)
    wqa = jnp.concatenate([wq[..., :QK_NOPE], q1, q2, zpad], -1).reshape(q_lora, H * MLA_QK).astype(BF16)
    wqb = jnp.concatenate([jnp.zeros_like(wq[..., :QK_NOPE]), -q2, q1, zpad], -1).reshape(
        q_lora, H * MLA_QK).astype(BF16)
    wkv = w_kv_up[0].reshape(kv_lora, H, QK_NOPE + V_DIM)
    wk = jnp.concatenate([wkv[..., :QK_NOPE], jnp.zeros((kv_lora, H, MLA_QK - QK_NOPE), F32)],
                         -1).reshape(kv_lora, H * MLA_QK).astype(BF16)
    wv = wkv[..., QK_NOPE:].reshape(kv_lora, H * V_DIM).astype(BF16)
    pr = np.zeros((LANE, H, MLA_QK), np.float32)
    for j in range(QK_ROPE):
        pr[j, :, QK_NOPE + j] = 1.0
        pr[QK_ROPE + j, :, QK_NOPE + j] = 1.0
    prope = jnp.asarray(pr.reshape(LANE, H * MLA_QK), BF16)

    inv = ROPE_THETA ** (-jnp.arange(half, dtype=F32) / half)
    ang = positions.astype(F32)[..., None] * inv
    cos, sin = jnp.cos(ang), jnp.sin(ang)
    one = jnp.ones((B, S, QK_NOPE), F32)
    zq = jnp.zeros((B, S, MLA_QK - QK_NOPE - QK_ROPE), F32)
    mla_scale = (QK_NOPE + QK_ROPE) ** -0.5
    cq_tab = jnp.concatenate([one, cos, cos, zq], -1) * mla_scale
    sq_tab = jnp.concatenate([jnp.zeros_like(one), sin, sin, zq], -1) * mla_scale
    ck_tab = jnp.concatenate([cos, cos, sin, sin, jnp.zeros((B, S, LANE - 4 * half), F32)], -1)

    row = lambda v: v.reshape(1, -1).astype(F32)
    (qT, kc, vc, ks, kw, vsT, vwT, gmT, qmT, km, vmT) = _inproj(
        x, row(norm1_g[0]), win_p, row(q_norm_g[0]), row(kv_norm_g[0]), wqa, wqb, wk, wv, prope,
        cq_tab, sq_tab, ck_tab)

    def chunks16(a):
        return a.reshape(B, nc, CMP_STRIDE, G, HEAD_DIM).transpose(0, 3, 1, 2, 4).reshape(
            B * G, nc, CMP_STRIDE * HEAD_DIM)

    def heads_first(a):
        return a.reshape(B, S, G, HEAD_DIM).transpose(0, 2, 1, 3)

    def lane_chunks(aT, nh, dh, ck):
        L = aT.shape[-1]
        return aT.reshape(B, nh, dh, L // ck, ck).transpose(0, 1, 3, 2, 4)

    c_all = jnp.stack([chunks16(kc), chunks16(vc)])
    pos_lo = cmp_pos[0, :CMP_STRIDE].reshape(1, -1)
    pos_hi = cmp_pos[0, CMP_STRIDE:].reshape(1, -1)
    split = CMP_STRIDE * HEAD_DIM
    w1 = jnp.stack([cmp_k_w1[0], cmp_v_w1[0]]).astype(BF16)
    w2 = jnp.stack([cmp_k_w2[0], cmp_v_w2[0]]).astype(BF16)
    cmp_out = _compress(c_all, pos_lo, pos_hi, w1[:, :split], w1[:, split:], w2)
    kcmp = cmp_out[0]
    vcmpT = cmp_out[1].transpose(0, 2, 1)

    ft, ts, tw = _bias_tables(t5_bias, nc)
    ocT, selb = _nsa_cmp(qT, kcmp, vcmpT, ft, _overlap_t(nc, n_cmp, nsel))
    osT = _nsa_sel(qT, heads_first(ks), lane_chunks(vsT, G, HEAD_DIM, TK), selb, ts)
    kwp = jnp.pad(heads_first(kw), ((0, 0), (0, 0), (WINDOW, 0), (0, 0)))
    vwTp = jnp.pad(vwT, ((0, 0), (0, 0), (WINDOW, 0)))
    owT = _nsa_win(qT, kwp, lane_chunks(vwTp, G, HEAD_DIM, TQ), tw)
    omT = _mla(qmT, km, lane_chunks(vmT, H, V_DIM, TK))

    o_gate = 4 * half
    x2, hnT = _outproj(ocT, osT, owT, omT, gmT[:, o_gate:o_gate + n_gate], x, row(grp_norm_nsa[0]),
                       row(grp_norm_mla[0]), w_out[0].astype(BF16), row(norm2_g[0]))

    s1p, s2, thr = _peer_topk(hnT, peer_wq[0].T.astype(BF16), peer_keys1[0].astype(BF16),
                              peer_keys2[0].astype(BF16))
    out = _peer_dense(hnT, peer_u[0].astype(BF16), peer_v[0].T.astype(BF16), s1p, s2, thr,
                      x2.reshape(B * S, D), row(final_g))
    return out.reshape(B, S, D)
```

```python
import functools
import math

import numpy as np
import jax
import jax.numpy as jnp
from jax import lax
from jax.experimental import pallas as pl
from jax.experimental.pallas import tpu as pltpu

F32 = jnp.float32
BF16 = jnp.bfloat16

HEAD_DIM = 64
NSA_HEADS = 8
NSA_KV_HEADS = 2
NSA_REP = NSA_HEADS // NSA_KV_HEADS
CMP_LEN = 32
CMP_STRIDE = 16
SEL_LEN = 64
SEL_TOPN = 16
WINDOW = 512
SEL_FORCE = 1e4
MLA_HEADS = 8
QK_NOPE = 64
QK_ROPE = 32
V_DIM = 64
ROPE_THETA = 10000.0
N_BUCKETS = 32
MAX_DISTANCE = 2048
PEER_HEADS = 8
PEER_TOPK = 16
EPS = 1e-6
NEG = -1e30
REMOVED = -3e38

LANE = 128
NSA_W = NSA_HEADS * HEAD_DIM
KV_W = NSA_KV_HEADS * HEAD_DIM
MLA_QK = 128
FAR_DIST = 2048

TM_IN = 512
TQ_CMP = 128
TQ = 512
TK = 256
TQW = 256
TM_OUT = 256
TM_TOPK = 256
TM_PEER = 512
NE_PEER = 512
SEL_WMAX = (FAR_DIST + TK - 1) // TK
SEL_ROFF = TK * SEL_WMAX + TQ - LANE
SEL_ROWS = SEL_ROFF + 2 * TK
WIN_ROWS = WINDOW + TQW
WIN_ROFF = TQW - LANE + WINDOW
WIN_TROWS = TQW - LANE + WIN_ROWS
LOG2E = math.log2(math.e)
V_ONES = 16
VMEM_LIMIT = 56 * 1024 * 1024


def _cparams(*sem):
    return pltpu.CompilerParams(dimension_semantics=sem, vmem_limit_bytes=VMEM_LIMIT)


def _rms(x):
    return x * lax.rsqrt(jnp.mean(x * x, axis=-1, keepdims=True) + EPS)


def _dot(a, b):
    return jnp.dot(a, b, preferred_element_type=F32)


def _inproj_kernel(x_ref, g1_ref, win_ref, qng_ref, kvng_ref, wqa_ref, wqb_ref, wk_ref, wv_ref,
                   prope_ref, cq_ref, sq_ref, ck_ref,
                   qT_ref, kc_ref, vc_ref, ks_ref, kw_ref, vsT_ref, vwT_ref, gT_ref,
                   qmT_ref, km_ref, vmT_ref):
    x = x_ref[0]
    h = _rms(x) * g1_ref[...]
    z = _dot(h.astype(BF16), win_ref[...])
    qT_ref[0] = (z[:, :NSA_W] * (HEAD_DIM ** -0.5 * LOG2E)).T.astype(BF16)
    o = NSA_W
    kc_ref[0] = z[:, o:o + KV_W]
    vc_ref[0] = z[:, o + KV_W:o + 2 * KV_W]
    ks_ref[0] = z[:, o + 2 * KV_W:o + 3 * KV_W].astype(BF16)
    vsT_ref[0] = z[:, o + 3 * KV_W:o + 4 * KV_W].T.astype(BF16)
    kw_ref[0] = z[:, o + 4 * KV_W:o + 5 * KV_W].astype(BF16)
    vwT_ref[0] = z[:, o + 5 * KV_W:o + 6 * KV_W].T.astype(BF16)
    o = o + 6 * KV_W
    cq = z[:, o:o + 256]
    ckv = z[:, o + 256:o + 384]
    misc = z[:, o + 384:o + 512]
    gT_ref[0] = jax.nn.sigmoid(misc).T
    cqn = (_rms(cq) * qng_ref[...]).astype(BF16)
    cos_q = jnp.tile(cq_ref[0], (1, MLA_HEADS))
    sin_q = jnp.tile(sq_ref[0], (1, MLA_HEADS))
    qm = _dot(cqn, wqa_ref[...]) * cos_q + _dot(cqn, wqb_ref[...]) * sin_q
    qmT_ref[0] = qm.T.astype(BF16)
    ckvn = (_rms(ckv) * kvng_ref[...]).astype(BF16)
    kr = (misc * ck_ref[0]).astype(BF16)
    km_ref[0] = (_dot(ckvn, wk_ref[...]) + _dot(kr, prope_ref[...])).astype(BF16)
    vmT_ref[0] = _dot(ckvn, wv_ref[...]).T.astype(BF16)


def _inproj(x, g1, win_p, qng, kvng, wqa, wqb, wk, wv, prope, cq_tab, sq_tab, ck_tab):
    B, S, D = x.shape
    tm = TM_IN
    full = lambda a: pl.BlockSpec(a.shape, lambda b, i: (0,) * a.ndim)
    tok = lambda w: pl.BlockSpec((1, tm, w), lambda b, i: (b, i, 0))
    tr = lambda w: pl.BlockSpec((1, w, tm), lambda b, i: (b, 0, i))
    outs = [
        (jax.ShapeDtypeStruct((B, NSA_W, S), BF16), tr(NSA_W)),
        (jax.ShapeDtypeStruct((B, S, KV_W), F32), tok(KV_W)),
        (jax.ShapeDtypeStruct((B, S, KV_W), F32), tok(KV_W)),
        (jax.ShapeDtypeStruct((B, S, KV_W), BF16), tok(KV_W)),
        (jax.ShapeDtypeStruct((B, S, KV_W), BF16), tok(KV_W)),
        (jax.ShapeDtypeStruct((B, KV_W, S), BF16), tr(KV_W)),
        (jax.ShapeDtypeStruct((B, KV_W, S), BF16), tr(KV_W)),
        (jax.ShapeDtypeStruct((B, LANE, S), F32), tr(LANE)),
        (jax.ShapeDtypeStruct((B, MLA_HEADS * MLA_QK, S), BF16), tr(MLA_HEADS * MLA_QK)),
        (jax.ShapeDtypeStruct((B, S, MLA_HEADS * MLA_QK), BF16), tok(MLA_HEADS * MLA_QK)),
        (jax.ShapeDtypeStruct((B, MLA_HEADS * V_DIM, S), BF16), tr(MLA_HEADS * V_DIM)),
    ]
    return pl.pallas_call(
        _inproj_kernel,
        grid=(B, S // tm),
        in_specs=[tok(D), full(g1), full(win_p), full(qng), full(kvng), full(wqa), full(wqb),
                  full(wk), full(wv), full(prope), tok(LANE), tok(LANE), tok(LANE)],
        out_specs=[o[1] for o in outs],
        out_shape=[o[0] for o in outs],
        compiler_params=_cparams("parallel", "parallel"),
        name="inproj",
    )(x, g1, win_p, qng, kvng, wqa, wqb, wk, wv, prope, cq_tab, sq_tab, ck_tab)


def _gelu(x):
    return 0.5 * x * (1.0 + lax.erf(x * (1.0 / math.sqrt(2.0))))


def _compress_kernel(c_ref, plo_ref, phi_ref, w1lo_ref, w1hi_ref, w2_ref, out_ref):
    c = c_ref[0, 0]
    nc = c.shape[0]
    a = _dot((c + plo_ref[...]).astype(BF16), w1lo_ref[0])
    b = _dot((c + phi_ref[...]).astype(BF16), w1hi_ref[0])
    hid = a + pltpu.roll(b, nc - 1, 0)
    out_ref[0, 0] = _dot(_gelu(hid).astype(BF16), w2_ref[0]).astype(BF16)


def _compress(c_all, pos_lo, pos_hi, w1lo, w1hi, w2):
    _, BG, nc, W = c_all.shape
    return pl.pallas_call(
        _compress_kernel,
        grid=(2, BG),
        in_specs=[pl.BlockSpec((1, 1, nc, W), lambda t, i: (t, i, 0, 0)),
                  pl.BlockSpec(pos_lo.shape, lambda t, i: (0, 0)),
                  pl.BlockSpec(pos_hi.shape, lambda t, i: (0, 0)),
                  pl.BlockSpec((1,) + w1lo.shape[1:], lambda t, i: (t, 0, 0)),
                  pl.BlockSpec((1,) + w1hi.shape[1:], lambda t, i: (t, 0, 0)),
                  pl.BlockSpec((1,) + w2.shape[1:], lambda t, i: (t, 0, 0))],
        out_specs=pl.BlockSpec((1, 1, nc, HEAD_DIM), lambda t, i: (t, i, 0, 0)),
        out_shape=jax.ShapeDtypeStruct((2, BG, nc, HEAD_DIM), BF16),
        compiler_params=_cparams("parallel", "parallel"),
        name="compress",
    )(c_all, pos_lo, pos_hi, w1lo, w1hi, w2)


def _nsa_cmp_kernel(qT_ref, kc_ref, vcT_ref, ft_ref, ovT_ref, ocT_ref, selb_ref):
    qi = pl.program_id(2)
    tq = qT_ref.shape[2]
    nc = kc_ref.shape[1]
    nsel = ovT_ref.shape[0]
    k = kc_ref[0]
    vT = vcT_ref[0]
    start = pl.multiple_of(nc - (tq // CMP_STRIDE) * qi, 8)
    psum = jnp.zeros((nc, tq), F32)
    for r in range(NSA_REP):
        s = _dot(k, qT_ref[0, r * HEAD_DIM:(r + 1) * HEAD_DIM, :])
        s = s + ft_ref[r, pl.ds(start, nc), :]
        m = jnp.max(s, axis=0, keepdims=True)
        p = jnp.exp2(s - m)
        l = jnp.sum(p, axis=0, keepdims=True)
        p = p * jnp.where(m > 0.5 * NEG, 1.0 / l, 0.0)
        ocT_ref[0, r * HEAD_DIM:(r + 1) * HEAD_DIM, :] = _dot(vT, p.astype(BF16))
        psum = psum + p
    p_hi = psum.astype(BF16)
    p_lo = (psum - p_hi.astype(F32)).astype(BF16)
    imp = _dot(ovT_ref[...], p_hi) + _dot(ovT_ref[...], p_lo)

    sid = lax.broadcasted_iota(jnp.int32, (nsel, tq), 0)
    t = qi * tq + lax.broadcasted_iota(jnp.int32, (nsel, tq), 1)
    cur = jnp.right_shift(t, int(math.log2(SEL_LEN)))
    valid = sid <= cur
    forced = (sid == 0) | (sid == cur) | (sid == cur - 1)
    n_free = SEL_TOPN - 1 - (cur[0:1] >= 1).astype(jnp.int32) - (cur[0:1] >= 2).astype(jnp.int32)
    work = jnp.where(valid & jnp.logical_not(forced), imp, NEG)
    tau = jnp.zeros((1, tq), F32)
    for it in range(1, SEL_TOPN):
        mx = jnp.max(work, axis=0, keepdims=True)
        if it >= SEL_TOPN - 3:
            tau = jnp.where(n_free == it, mx, tau)
        if it < SEL_TOPN - 1:
            work = jnp.where(work >= mx, REMOVED, work)
    sel = valid & (forced | (imp >= tau))
    selb_ref[0, 0] = jnp.where(sel, 0.0, NEG)


def _nsa_cmp(qT, kcmp, vcmpT, ft, ovT):
    B, _, S = qT.shape
    G = NSA_KV_HEADS
    nc = kcmp.shape[1]
    nsel = ovT.shape[0]
    tq = TQ_CMP
    gw = NSA_REP * HEAD_DIM
    return pl.pallas_call(
        _nsa_cmp_kernel,
        grid=(B, G, S // tq),
        in_specs=[pl.BlockSpec((1, gw, tq), lambda b, g, i: (b, g, i)),
                  pl.BlockSpec((1, nc, HEAD_DIM), lambda b, g, i: (b * G + g, 0, 0)),
                  pl.BlockSpec((1, HEAD_DIM, nc), lambda b, g, i: (b * G + g, 0, 0)),
                  pl.BlockSpec((NSA_REP, 2 * nc, tq), lambda b, g, i: (g, 0, 0)),
                  pl.BlockSpec(ovT.shape, lambda b, g, i: (0, 0))],
        out_specs=[pl.BlockSpec((1, gw, tq), lambda b, g, i: (b, g, i)),
                   pl.BlockSpec((1, 1, nsel, tq), lambda b, g, i: (b, g, 0, i))],
        out_shape=[jax.ShapeDtypeStruct((B, NSA_W, S), F32),
                   jax.ShapeDtypeStruct((B, G, nsel, S), F32)],
        compiler_params=_cparams("parallel", "parallel", "parallel"),
        name="nsa_cmp",
    )(qT, kcmp, vcmpT, ft, ovT)


def _sub_all(x8, op):
    for sh in (4, 2, 1):
        x8 = op(x8, pltpu.roll(x8, sh, 0))
    return x8


def _col_max8(s):
    return jnp.max(s.reshape(s.shape[0] // 8, 8, s.shape[1]), axis=0)


def _fa_init(m_ref, al_ref, acc_ref):
    m_ref[...] = jnp.full(m_ref.shape, REMOVED, F32)
    al_ref[...] = jnp.ones(al_ref.shape, F32)
    acc_ref[...] = jnp.zeros(acc_ref.shape, F32)


def _fa_update_m(st, idx, mx8):
    m_ref, al_ref = st[0], st[1]
    m_old = m_ref[idx]
    m_new = jnp.maximum(m_old, _sub_all(mx8, jnp.maximum))
    al_ref[idx] = jnp.exp2(m_old - m_new)
    m_ref[idx] = m_new


def _fa_pass2(s_buf, vT, st, idx):
    m_ref, al_ref, acc_ref = st
    tk, tq = s_buf.shape
    dv = acc_ref.shape[1]
    p3 = jnp.exp2(s_buf[...].reshape(tk // 8, 8, tq) - m_ref[idx][None])
    pv = _dot(vT, p3.reshape(tk, tq).astype(BF16))
    acc_ref[idx] = (acc_ref[idx].reshape(dv // 8, 8, tq) * al_ref[idx][None]).reshape(dv, tq) + pv


def _fa_result(st, idx, dv):
    acc = st[2][idx]
    tq = acc.shape[1]
    inv = 1.0 / acc[dv:dv + 8]
    return (acc[:dv].reshape(dv // 8, 8, tq) * inv[None]).reshape(dv, tq)


def _with_ones_rows(vT5):
    return jnp.concatenate([vT5, jnp.ones(vT5.shape[:-2] + (V_ONES, vT5.shape[-1]), vT5.dtype)], axis=-2)


def _fa_segment(c0, n, pass1, pass2, update):
    last = c0 + n - 1
    update(pass1(c0, 0))

    def body(i, carry):
        c = c0 + 2 * i
        mx = pass1(c + 1, 1)
        pass2(c, 0)
        update(mx)
        mx = pass1(jnp.where(c + 2 <= last, c + 2, c), 0)
        pass2(c + 1, 1)
        update(mx)
        return carry

    lax.fori_loop(0, n // 2, body, 0)


def _nsa_sel_kernel(qT_ref, ks_ref, vsT_ref, selb_ref, ts_ref, osT_ref, s_ref, m_ref, al_ref, acc_ref):
    qi = pl.program_id(2)
    tq = qT_ref.shape[2]
    st = (m_ref, al_ref, acc_ref)
    _fa_init(*st)
    blocks = TK // SEL_LEN

    def make_pass1(near):
        def pass1(c, slot):
            k = ks_ref[0, 0, pl.ds(pl.multiple_of(c * TK, TK), TK), :]
            mb = selb_ref[0, 0, pl.ds(pl.multiple_of((c // 2) * 2 * blocks, 2 * blocks), 2 * blocks), :]
            mb = mb[slot * blocks:(slot + 1) * blocks]
            base = SEL_ROFF - TK * (2 * qi - c)
            out = []
            for r in range(NSA_REP):
                s = _dot(k, qT_ref[0, r * HEAD_DIM:(r + 1) * HEAD_DIM, :])
                if near:
                    s = s + jnp.concatenate(
                        [ts_ref[r, pl.ds(pl.multiple_of(base - LANE * a, LANE), TK), :]
                         for a in range(tq // LANE)], axis=1)
                s = jnp.concatenate(
                    [s[j * SEL_LEN:(j + 1) * SEL_LEN] + mb[j:j + 1] for j in range(blocks)], axis=0)
                s_ref[slot, r] = s
                out.append(_col_max8(s))
            return out
        return pass1

    def pass2(c, slot):
        vT = vsT_ref[0, 0, c]
        for r in range(NSA_REP):
            _fa_pass2(s_ref.at[slot, r], vT, st, r)

    def update(mx):
        for r in range(NSA_REP):
            _fa_update_m(st, r, mx[r])

    n_far = jnp.maximum(0, 2 * qi - SEL_WMAX)

    @pl.when(n_far > 0)
    def _():
        _fa_segment(0, n_far, make_pass1(False), pass2, update)

    _fa_segment(n_far, 2 * qi + 2 - n_far, make_pass1(True), pass2, update)
    for r in range(NSA_REP):
        osT_ref[0, r * HEAD_DIM:(r + 1) * HEAD_DIM, :] = _fa_result(st, r, HEAD_DIM)


def _nsa_sel(qT, ks4, vsT5, selb, ts):
    B, _, S = qT.shape
    G = NSA_KV_HEADS
    nsel = selb.shape[2]
    gw = NSA_REP * HEAD_DIM
    return pl.pallas_call(
        _nsa_sel_kernel,
        grid=(B, G, S // TQ),
        in_specs=[pl.BlockSpec((1, gw, TQ), lambda b, g, i: (b, g, i)),
                  pl.BlockSpec((1, 1, S, HEAD_DIM), lambda b, g, i: (b, g, 0, 0)),
                  pl.BlockSpec((1, 1, S // TK, HEAD_DIM + V_ONES, TK), lambda b, g, i: (b, g, 0, 0, 0)),
                  pl.BlockSpec((1, 1, nsel, TQ), lambda b, g, i: (b, g, 0, i)),
                  pl.BlockSpec((NSA_REP,) + ts.shape[1:], lambda b, g, i: (g, 0, 0))],
        out_specs=pl.BlockSpec((1, gw, TQ), lambda b, g, i: (b, g, i)),
        out_shape=jax.ShapeDtypeStruct((B, NSA_W, S), F32),
        scratch_shapes=[pltpu.VMEM((2, NSA_REP, TK, TQ), F32)]
        + [pltpu.VMEM((NSA_REP, 8, TQ), F32)] * 2 + [pltpu.VMEM((NSA_REP, HEAD_DIM + V_ONES, TQ), F32)],
        compiler_params=_cparams("parallel", "parallel", "parallel"),
        name="nsa_sel",
    )(qT, ks4, vsT5, selb, ts)


def _nsa_win_kernel(qT_ref, kw_ref, vwT_ref, tw_ref, owT_ref):
    qi = pl.program_id(2)
    tq = qT_ref.shape[2]
    k = kw_ref[0, 0, pl.ds(pl.multiple_of(qi * TQW, TQW), WIN_ROWS), :]
    row = lax.broadcasted_iota(jnp.int32, (WIN_ROWS, tq), 0)
    in_seq = row >= WINDOW - qi * TQW
    for r in range(NSA_REP):
        s = _dot(k, qT_ref[0, r * HEAD_DIM:(r + 1) * HEAD_DIM, :])
        halves = [tw_ref[r, LANE * (tq // LANE - 1 - a):LANE * (tq // LANE - 1 - a) + WIN_ROWS, :]
                  for a in range(tq // LANE)]
        s = jnp.where(in_seq, s + jnp.concatenate(halves, axis=1), NEG)
        m = jnp.max(s, axis=0, keepdims=True)
        p = jnp.exp2(s - m)
        l = jnp.sum(p, axis=0, keepdims=True)
        pb = p.astype(BF16)
        acc = jnp.zeros((HEAD_DIM, tq), F32)
        for j in range(WIN_ROWS // TQW):
            acc = acc + _dot(vwT_ref[0, 0, qi + j], pb[j * TQW:(j + 1) * TQW])
        owT_ref[0, r * HEAD_DIM:(r + 1) * HEAD_DIM, :] = acc / l


def _nsa_win(qT, kwp4, vwT5, tw):
    B, _, S = qT.shape
    G = NSA_KV_HEADS
    gw = NSA_REP * HEAD_DIM
    return pl.pallas_call(
        _nsa_win_kernel,
        grid=(B, G, S // TQW),
        in_specs=[pl.BlockSpec((1, gw, TQW), lambda b, g, i: (b, g, i)),
                  pl.BlockSpec((1, 1) + kwp4.shape[2:], lambda b, g, i: (b, g, 0, 0)),
                  pl.BlockSpec((1, 1) + vwT5.shape[2:], lambda b, g, i: (b, g, 0, 0, 0)),
                  pl.BlockSpec((NSA_REP,) + tw.shape[1:], lambda b, g, i: (g, 0, 0))],
        out_specs=pl.BlockSpec((1, gw, TQW), lambda b, g, i: (b, g, i)),
        out_shape=jax.ShapeDtypeStruct((B, NSA_W, S), F32),
        compiler_params=_cparams("parallel", "parallel", "parallel"),
        name="nsa_win",
    )(qT, kwp4, vwT5, tw)


MLA_HPS = 2


def _mla_kernel(qT_ref, k_ref, vT_ref, oT_ref, s_ref, m_ref, al_ref, acc_ref):
    qi = pl.program_id(2)
    tq = qT_ref.shape[2]
    st = (m_ref, al_ref, acc_ref)
    _fa_init(*st)

    def make_pass1(masked):
        def pass1(c, slot):
            out = []
            for h in range(MLA_HPS):
                k = k_ref[0, pl.ds(pl.multiple_of(c * TK, TK), TK), h * MLA_QK:(h + 1) * MLA_QK]
                s = _dot(k, qT_ref[0, h * MLA_QK:(h + 1) * MLA_QK, :])
                if masked:
                    kpos = c * TK + lax.broadcasted_iota(jnp.int32, (TK, tq), 0)
                    qpos = qi * TQ + lax.broadcasted_iota(jnp.int32, (TK, tq), 1)
                    s = jnp.where(kpos <= qpos, s, NEG)
                s_ref[slot, h] = s
                out.append(_col_max8(s))
            return out
        return pass1

    def pass2(c, slot):
        for h in range(MLA_HPS):
            _fa_pass2(s_ref.at[slot, h], vT_ref[0, h, c], st, h)

    def update(mx):
        for h in range(MLA_HPS):
            _fa_update_m(st, h, mx[h])

    @pl.when(qi > 0)
    def _():
        _fa_segment(0, 2 * qi, make_pass1(False), pass2, update)

    diag = make_pass1(True)
    update(diag(2 * qi, 0))
    mx = diag(2 * qi + 1, 1)
    pass2(2 * qi, 0)
    update(mx)
    pass2(2 * qi + 1, 1)
    for h in range(MLA_HPS):
        oT_ref[0, h * V_DIM:(h + 1) * V_DIM, :] = _fa_result(st, h, V_DIM)


def _mla(qmT, km, vmT5):
    B, _, S = qmT.shape
    hp = MLA_HPS
    return pl.pallas_call(
        _mla_kernel,
        grid=(B, MLA_HEADS // hp, S // TQ),
        in_specs=[pl.BlockSpec((1, hp * MLA_QK, TQ), lambda b, h, i: (b, h, i)),
                  pl.BlockSpec((1, S, hp * MLA_QK), lambda b, h, i: (b, 0, h)),
                  pl.BlockSpec((1, hp, S // TK, V_DIM + V_ONES, TK), lambda b, h, i: (b, h, 0, 0, 0))],
        out_specs=pl.BlockSpec((1, hp * V_DIM, TQ), lambda b, h, i: (b, h, i)),
        out_shape=jax.ShapeDtypeStruct((B, MLA_HEADS * V_DIM, S), F32),
        scratch_shapes=[pltpu.VMEM((2, hp, TK, TQ), F32)] + [pltpu.VMEM((hp, 8, TQ), F32)] * 2
        + [pltpu.VMEM((hp, V_DIM + V_ONES, TQ), F32)],
        compiler_params=_cparams("parallel", "parallel", "parallel"),
        name="mla",
    )(qmT, km, vmT5)


def _outproj_kernel(ocT_ref, osT_ref, owT_ref, omT_ref, gT_ref, x_ref, gn_ref, gm_ref, wout_ref,
                    g2_ref, x2_ref, hnT_ref):
    parts = []
    for h in range(NSA_HEADS):
        rows = slice(h * HEAD_DIM, (h + 1) * HEAD_DIM)
        parts.append(gT_ref[0, 3 * h:3 * h + 1, :] * ocT_ref[0, rows, :]
                     + gT_ref[0, 3 * h + 1:3 * h + 2, :] * osT_ref[0, rows, :]
                     + gT_ref[0, 3 * h + 2:3 * h + 3, :] * owT_ref[0, rows, :])
    nsaT = jnp.concatenate(parts, axis=0)

    def norm_t(yT, g):
        y = yT * lax.rsqrt(jnp.mean(yT * yT, axis=0, keepdims=True) + EPS)
        return (y.T * g).astype(BF16)

    y_nsa = norm_t(nsaT, gn_ref[...])
    y_mla = norm_t(omT_ref[0], gm_ref[...])
    x2 = x_ref[0] + _dot(y_nsa, wout_ref[:NSA_W, :]) + _dot(y_mla, wout_ref[NSA_W:, :])
    x2_ref[0] = x2
    hnT_ref[...] = (_rms(x2) * g2_ref[...]).T.astype(BF16)


def _outproj(ocT, osT, owT, omT, gT, x, gn, gm, wout, g2):
    B, S, D = x.shape
    tm = TM_OUT
    nt = S // tm
    tr = lambda a: pl.BlockSpec((1, a.shape[1], tm), lambda b, i: (b, 0, i))
    full = lambda a: pl.BlockSpec(a.shape, lambda b, i: (0,) * a.ndim)
    return pl.pallas_call(
        _outproj_kernel,
        grid=(B, nt),
        in_specs=[tr(ocT), tr(osT), tr(owT), tr(omT), tr(gT),
                  pl.BlockSpec((1, tm, D), lambda b, i: (b, i, 0)),
                  full(gn), full(gm), full(wout), full(g2)],
        out_specs=[pl.BlockSpec((1, tm, D), lambda b, i: (b, i, 0)),
                   pl.BlockSpec((D, tm), lambda b, i: (0, b * nt + i))],
        out_shape=[jax.ShapeDtypeStruct((B, S, D), F32),
                   jax.ShapeDtypeStruct((D, B * S), BF16)],
        compiler_params=_cparams("parallel", "parallel"),
        name="outproj",
    )(ocT, osT, owT, omT, gT, x, gn, gm, wout, g2)


def _row_max_bcast(w3):
    m8 = jnp.max(w3, axis=0)
    for sh in (4, 2, 1):
        m8 = jnp.maximum(m8, pltpu.roll(m8, sh, 0))
    return m8


def _top_values(s, n):
    w3 = s.reshape(s.shape[0] // 8, 8, s.shape[1])
    vals = []
    for it in range(n):
        mx = _row_max_bcast(w3)
        vals.append(mx)
        if it < n - 1:
            w3 = jnp.where(w3 >= mx[None], REMOVED, w3)
    return vals


def _pair_list(n):
    return [(i, j) for i in range(n) for j in range(n) if (i + 1) * (j + 1) <= n]


def _top_pair_sums(a, b, n):
    cands = [a[i] + b[j] for i, j in _pair_list(n)]
    vals = []
    for it in range(n):
        mx = functools.reduce(jnp.maximum, cands)
        vals.append(mx)
        if it < n - 1:
            cands = [jnp.where(c >= mx, REMOVED, c) for c in cands]
    return vals


def _peer_topk_kernel(hnT_ref, wqT_ref, k1_ref, k2_ref, c_ref, s2_ref, e1_ref, b1_ref):
    tm = hnT_ref.shape[1]
    n = PEER_TOPK + 1
    qT = _dot(wqT_ref[...], hnT_ref[...]).astype(BF16)
    dk = k1_ref.shape[1]
    sub = lax.broadcasted_iota(jnp.int32, (8, tm), 0)
    a_m = [jnp.zeros((8, tm), F32)] * n
    b_m = [jnp.zeros((8, tm), F32)] * n
    for h in range(PEER_HEADS):
        s1 = _dot(k1_ref[...], qT[2 * h * dk:(2 * h + 1) * dk])
        s2 = _dot(k2_ref[...], qT[(2 * h + 1) * dk:(2 * h + 2) * dk])
        c_ref[h] = s1
        s2_ref[h] = s2
        a = _top_values(s1, n)
        b = _top_values(s2, n)
        a_m = [jnp.where(sub == h, a[i], a_m[i]) for i in range(n)]
        b_m = [jnp.where(sub == h, b[i], b_m[i]) for i in range(n)]
    v = _top_pair_sums(a_m, b_m, n)
    z = functools.reduce(jnp.add, [jnp.exp(v[i] - v[0]) for i in range(PEER_TOPK)])
    thr = 0.5 * (v[PEER_TOPK - 1] + v[PEER_TOPK])
    for h in range(PEER_HEADS):
        c_ref[h] = thr[h:h + 1, :] - c_ref[h]
    e1_ref[...] = thr - a_m[0] - jnp.log(z)
    b1_ref[...] = b_m[0]


def _peer_topk(hnT, wqT, k1, k2):
    D, N = hnT.shape
    tm = TM_TOPK
    H = PEER_HEADS
    nk = k1.shape[0]
    full = lambda a: pl.BlockSpec(a.shape, lambda i: (0,) * a.ndim)
    return pl.pallas_call(
        _peer_topk_kernel,
        grid=(N // tm,),
        in_specs=[pl.BlockSpec((D, tm), lambda i: (0, i)), full(wqT), full(k1), full(k2)],
        out_specs=[pl.BlockSpec((H, nk, tm), lambda i: (0, 0, i)),
                   pl.BlockSpec((H, nk, tm), lambda i: (0, 0, i)),
                   pl.BlockSpec((H, tm), lambda i: (0, i)),
                   pl.BlockSpec((H, tm), lambda i: (0, i))],
        out_shape=[jax.ShapeDtypeStruct((H, nk, N), F32),
                   jax.ShapeDtypeStruct((H, nk, N), F32),
                   jax.ShapeDtypeStruct((H, N), F32),
                   jax.ShapeDtypeStruct((H, N), F32)],
        compiler_params=_cparams("parallel"),
        name="peer_topk",
    )(hnT, wqT, k1, k2)


def _peer_dense_kernel(hnT_ref, u_ref, vT_ref, c_ref, s2_ref, e1_ref, b1_ref, x2_ref, gf_ref, out_ref,
                       acc_ref, p1_ref, p2_ref):
    j = pl.program_id(1)
    nk = s2_ref.shape[1]
    ne = u_ref.shape[0]
    tm = hnT_ref.shape[1]

    @pl.when(j == 0)
    def _():
        acc_ref[...] = jnp.zeros(acc_ref.shape, F32)
        for h in range(PEER_HEADS):
            p1_ref[h] = jnp.exp(e1_ref[h:h + 1, :] - c_ref[h])
            p2_ref[h] = jnp.exp(s2_ref[h] - b1_ref[h:h + 1, :])

    act = _dot(u_ref[...], hnT_ref[...])
    ws = []
    for kk in range(ne // nk):
        i1 = j * (ne // nk) + kk
        gate = jnp.zeros((nk, tm), F32)
        for h in range(PEER_HEADS):
            sel = s2_ref[h] >= c_ref[h, pl.ds(i1, 1), :]
            gate = gate + jnp.where(sel, p1_ref[h, pl.ds(i1, 1), :] * p2_ref[h], 0.0)
        ws.append((gate * _gelu(act[kk * nk:(kk + 1) * nk])).astype(BF16))
    acc_ref[...] += _dot(vT_ref[...], jnp.concatenate(ws, axis=0))

    @pl.when(j == pl.num_programs(1) - 1)
    def _():
        y = x2_ref[...] + acc_ref[...].T
        out_ref[...] = _rms(y) * gf_ref[...]


def _peer_dense(hnT, u, vT, c, s2, e1, b1, x2, gf):
    D, N = hnT.shape
    NE = u.shape[0]
    tm, ne = TM_PEER, NE_PEER
    H, nk, _ = s2.shape
    return pl.pallas_call(
        _peer_dense_kernel,
        grid=(N // tm, NE // ne),
        in_specs=[pl.BlockSpec((D, tm), lambda i, j: (0, i)),
                  pl.BlockSpec((ne, D), lambda i, j: (j, 0)),
                  pl.BlockSpec((D, ne), lambda i, j: (0, j)),
                  pl.BlockSpec((H, nk, tm), lambda i, j: (0, 0, i)),
                  pl.BlockSpec((H, nk, tm), lambda i, j: (0, 0, i)),
                  pl.BlockSpec((H, tm), lambda i, j: (0, i)),
                  pl.BlockSpec((H, tm), lambda i, j: (0, i)),
                  pl.BlockSpec((tm, D), lambda i, j: (i, 0)),
                  pl.BlockSpec(gf.shape, lambda i, j: (0, 0))],
        out_specs=pl.BlockSpec((tm, D), lambda i, j: (i, 0)),
        out_shape=jax.ShapeDtypeStruct((N, D), F32),
        scratch_shapes=[pltpu.VMEM((D, tm), F32), pltpu.VMEM((H, nk, tm), F32),
                        pltpu.VMEM((H, nk, tm), F32)],
        compiler_params=_cparams("parallel", "arbitrary"),
        name="peer_dense",
    )(hnT, u, vT, c, s2, e1, b1, x2, gf)


def _t5_bucket(dist):
    dist = jnp.maximum(dist, 0)
    max_exact = N_BUCKETS // 2
    d = jnp.maximum(dist, 1).astype(F32)
    large = max_exact + (jnp.log(d / max_exact) / math.log(MAX_DISTANCE / max_exact)
                         * (N_BUCKETS - max_exact)).astype(jnp.int32)
    return jnp.where(dist < max_exact, dist, jnp.minimum(large, N_BUCKETS - 1))


def _toeplitz(f, rows):
    H = f.shape[0]
    nb = rows // LANE
    f = jnp.pad(f, ((0, 0), (0, (nb + 1) * LANE - f.shape[1])))
    seg = f.reshape(H, nb + 1, LANE)
    seg = jnp.concatenate([seg[:, :-1], seg[:, 1:], jnp.zeros((H, nb, 1), F32)], axis=-1)
    hank = jnp.tile(seg, (1, 1, LANE + 1))[..., :LANE * (2 * LANE + 2)]
    hank = hank.reshape(H, nb, LANE, 2 * LANE + 2)[..., :LANE]
    return hank.reshape(H, rows, LANE)[:, ::-1, :]


def _bias_tables(t5_bias, nc):
    onehot = lambda d: (_t5_bucket(d)[:, None] == jnp.arange(N_BUCKETS)[None, :]).astype(F32)

    def rel(d):
        return jnp.einsum("nb,bh->hn", onehot(d), t5_bias.astype(F32), precision=lax.Precision.HIGHEST)

    far = t5_bias[N_BUCKETS - 1].astype(F32)[:, None]
    x = jnp.arange(CMP_STRIDE * (2 * nc - 1) + LANE)
    d = x - CMP_STRIDE * (nc - 1) - (CMP_LEN - 1)
    f16 = jnp.where(d >= 0, rel(d) * LOG2E, NEG).reshape(t5_bias.shape[1], -1, CMP_STRIDE)
    ft = jnp.concatenate([f16[:, k:k + 2 * nc] for k in range(LANE // CMP_STRIDE)], axis=-1)[:, ::-1, :]
    d = jnp.arange(SEL_ROWS + LANE - 1) - (SEL_ROWS - 1) + SEL_ROFF
    ts = _toeplitz(jnp.where(d >= 0, (rel(d) - far) * LOG2E, NEG), SEL_ROWS)
    d = jnp.arange(WIN_TROWS + LANE - 1) - (WIN_TROWS - 1) + WIN_ROFF
    tw = _toeplitz(jnp.where((d >= 0) & (d < WINDOW), rel(d) * LOG2E, NEG), WIN_TROWS)
    return ft, ts, tw


def _overlap_t(nc, n_cmp, nsel):
    cs = np.arange(nc) * CMP_STRIDE
    ce = cs + CMP_LEN - 1
    ss = np.arange(nsel) * SEL_LEN
    ov = (cs[None, :] < ss[:, None] + SEL_LEN) & (ce[None, :] >= ss[:, None]) & (np.arange(nc)[None, :] < n_cmp)
    return jnp.asarray(ov, BF16)


def kernel(x, positions, norm1_g, w_in, cmp_pos, cmp_k_w1, cmp_k_w2, cmp_v_w1, cmp_v_w2, t5_bias,
           q_norm_g, w_q_up, kv_norm_g, w_kv_up, grp_norm_nsa, grp_norm_mla, w_out, norm2_g,
           peer_wq, peer_keys1, peer_keys2, peer_u, peer_v, final_g):
    B, S, D = x.shape
    G, H = NSA_KV_HEADS, MLA_HEADS
    assert S % TQ == 0 and S % TM_IN == 0 and norm1_g.shape[0] == 1
    nc = S // CMP_STRIDE
    n_cmp = (S - CMP_LEN) // CMP_STRIDE + 1
    nsel = S // SEL_LEN
    half = QK_ROPE // 2

    w = w_in[0]
    o_g = NSA_W + 6 * KV_W
    n_gate = 3 * NSA_HEADS
    o_cq = o_g + n_gate
    q_lora = w_q_up.shape[1]
    kv_lora = w_kv_up.shape[1]
    o_ckv = o_cq + q_lora
    o_kr = o_ckv + kv_lora
    kr1, kr2 = w[:, o_kr:o_kr + half], w[:, o_kr + half:o_kr + 2 * half]
    misc = jnp.concatenate([kr1, kr2, -kr2, kr1, w[:, o_g:o_g + n_gate],
                            jnp.zeros((D, LANE - 4 * half - n_gate), F32)], axis=1)
    win_p = jnp.concatenate([w[:, :o_g], w[:, o_cq:o_kr], misc], axis=1).astype(BF16)

    wq = w_q_up[0].reshape(q_lora, H, QK_NOPE + QK_ROPE)
    q1, q2 = wq[..., QK_NOPE:QK_NOPE + half], wq[..., QK_NOPE + half:]
    zpad = jnp.zeros((q_lora, H, MLA_QK - QK_NOPE - QK_ROPE), F32)
    wqa = jnp.concatenate([wq[..., :QK_NOPE], q1, q2, zpad], -1).reshape(q_lora, H * MLA_QK).astype(BF16)
    wqb = jnp.concatenate([jnp.zeros_like(wq[..., :QK_NOPE]), -q2, q1, zpad], -1).reshape(
        q_lora, H * MLA_QK).astype(BF16)
    wkv = w_kv_up[0].reshape(kv_lora, H, QK_NOPE + V_DIM)
    wk = jnp.concatenate([wkv[..., :QK_NOPE], jnp.zeros((kv_lora, H, MLA_QK - QK_NOPE), F32)],
                         -1).reshape(kv_lora, H * MLA_QK).astype(BF16)
    wv = wkv[..., QK_NOPE:].reshape(kv_lora, H * V_DIM).astype(BF16)
    pr = np.zeros((LANE, H, MLA_QK), np.float32)
    for j in range(QK_ROPE):
        pr[j, :, QK_NOPE + j] = 1.0
        pr[QK_ROPE + j, :, QK_NOPE + j] = 1.0
    prope = jnp.asarray(pr.reshape(LANE, H * MLA_QK), BF16)

    inv = ROPE_THETA ** (-jnp.arange(half, dtype=F32) / half)
    ang = positions.astype(F32)[..., None] * inv
    cos, sin = jnp.cos(ang), jnp.sin(ang)
    one = jnp.ones((B, S, QK_NOPE), F32)
    zq = jnp.zeros((B, S, MLA_QK - QK_NOPE - QK_ROPE), F32)
    mla_scale = (QK_NOPE + QK_ROPE) ** -0.5 * LOG2E
    cq_tab = jnp.concatenate([one, cos, cos, zq], -1) * mla_scale
    sq_tab = jnp.concatenate([jnp.zeros_like(one), sin, sin, zq], -1) * mla_scale
    ck_tab = jnp.concatenate([cos, cos, sin, sin, jnp.zeros((B, S, LANE - 4 * half), F32)], -1)

    row = lambda v: v.reshape(1, -1).astype(F32)
    (qT, kc, vc, ks, kw, vsT, vwT, gmT, qmT, km, vmT) = _inproj(
        x, row(norm1_g[0]), win_p, row(q_norm_g[0]), row(kv_norm_g[0]), wqa, wqb, wk, wv, prope,
        cq_tab, sq_tab, ck_tab)

    def chunks16(a):
        return a.reshape(B, nc, CMP_STRIDE, G, HEAD_DIM).transpose(0, 3, 1, 2, 4).reshape(
            B * G, nc, CMP_STRIDE * HEAD_DIM)

    def heads_first(a):
        return a.reshape(B, S, G, HEAD_DIM).transpose(0, 2, 1, 3)

    def lane_chunks(aT, nh, dh, ck):
        L = aT.shape[-1]
        return aT.reshape(B, nh, dh, L // ck, ck).transpose(0, 1, 3, 2, 4)

    c_all = jnp.stack([chunks16(kc), chunks16(vc)])
    pos_lo = cmp_pos[0, :CMP_STRIDE].reshape(1, -1)
    pos_hi = cmp_pos[0, CMP_STRIDE:].reshape(1, -1)
    split = CMP_STRIDE * HEAD_DIM
    w1 = jnp.stack([cmp_k_w1[0], cmp_v_w1[0]]).astype(BF16)
    w2 = jnp.stack([cmp_k_w2[0], cmp_v_w2[0]]).astype(BF16)
    cmp_out = _compress(c_all, pos_lo, pos_hi, w1[:, :split], w1[:, split:], w2)
    kcmp = cmp_out[0]
    vcmpT = cmp_out[1].transpose(0, 2, 1)

    ft, ts, tw = _bias_tables(t5_bias, nc)
    ocT, selb = _nsa_cmp(qT, kcmp, vcmpT, ft, _overlap_t(nc, n_cmp, nsel))
    osT = _nsa_sel(qT, heads_first(ks), _with_ones_rows(lane_chunks(vsT, G, HEAD_DIM, TK)), selb, ts)
    kwp = jnp.pad(heads_first(kw), ((0, 0), (0, 0), (WINDOW, 0), (0, 0)))
    vwTp = jnp.pad(vwT, ((0, 0), (0, 0), (WINDOW, 0)))
    owT = _nsa_win(qT, kwp, lane_chunks(vwTp, G, HEAD_DIM, TQW), tw)
    omT = _mla(qmT, km, _with_ones_rows(lane_chunks(vmT, H, V_DIM, TK)))

    o_gate = 4 * half
    x2, hnT = _outproj(ocT, osT, owT, omT, gmT[:, o_gate:o_gate + n_gate], x, row(grp_norm_nsa[0]),
                       row(grp_norm_mla[0]), w_out[0].astype(BF16), row(norm2_g[0]))

    c, s2, e1, b1 = _peer_topk(hnT, peer_wq[0].T.astype(BF16), peer_keys1[0].astype(BF16),
                               peer_keys2[0].astype(BF16))
    out = _peer_dense(hnT, peer_u[0].astype(BF16), peer_v[0].T.astype(BF16), c, s2, e1, b1,
                      x2.reshape(B * S, D), row(final_g))
    return out.reshape(B, S, D)
```

```python
import functools
import math

import numpy as np
import jax
import jax.numpy as jnp
from jax import lax
from jax.experimental import pallas as pl
from jax.experimental.pallas import tpu as pltpu

F32 = jnp.float32
BF16 = jnp.bfloat16

HEAD_DIM = 64
NSA_HEADS = 8
NSA_KV_HEADS = 2
NSA_REP = NSA_HEADS // NSA_KV_HEADS
CMP_LEN = 32
CMP_STRIDE = 16
SEL_LEN = 64
SEL_TOPN = 16
WINDOW = 512
SEL_FORCE = 1e4
MLA_HEADS = 8
QK_NOPE = 64
QK_ROPE = 32
V_DIM = 64
ROPE_THETA = 10000.0
N_BUCKETS = 32
MAX_DISTANCE = 2048
PEER_HEADS = 8
PEER_TOPK = 16
EPS = 1e-6
NEG = -1e30
REMOVED = -3e38

LANE = 128
NSA_W = NSA_HEADS * HEAD_DIM
KV_W = NSA_KV_HEADS * HEAD_DIM
MLA_QK = 128
FAR_DIST = 2048

TM_IN = 512
TQ_CMP = 256
TQ = 512
TK = 256
TM_OUT = 256
TM_TOPK = 256
TM_PEER = 512
NE_PEER = 512
SEL_WMAX = (FAR_DIST + TK - 1) // TK
SEL_ROFF = TK * SEL_WMAX + TQ - LANE
SEL_ROWS = SEL_ROFF + 2 * TK
WIN_CHUNKS_BACK = WINDOW // TK
WIN_ROFF2 = TK * WIN_CHUNKS_BACK + TQ - LANE
WIN_ROWS2 = WIN_ROFF2 + 2 * TK
LOG2E = math.log2(math.e)
V_ONES = 16
VMEM_LIMIT = 56 * 1024 * 1024


def _cparams(*sem):
    return pltpu.CompilerParams(dimension_semantics=sem, vmem_limit_bytes=VMEM_LIMIT)


def _rms(x):
    return x * lax.rsqrt(jnp.mean(x * x, axis=-1, keepdims=True) + EPS)


def _dot(a, b):
    return jnp.dot(a, b, preferred_element_type=F32)


def _inproj_kernel(x_ref, g1_ref, win_ref, qng_ref, kvng_ref, wqa_ref, wqb_ref, wk_ref, wv_ref,
                   prope_ref, cq_ref, sq_ref, ck_ref,
                   qT_ref, kc_ref, vc_ref, ks_ref, kw_ref, vsT_ref, vwT_ref, gT_ref,
                   qmT_ref, km_ref, vmT_ref):
    x = x_ref[0]
    h = _rms(x) * g1_ref[...]
    z = _dot(h.astype(BF16), win_ref[...])
    qT_ref[0] = (z[:, :NSA_W] * (HEAD_DIM ** -0.5 * LOG2E)).T.astype(BF16)
    o = NSA_W
    kc_ref[0] = z[:, o:o + KV_W]
    vc_ref[0] = z[:, o + KV_W:o + 2 * KV_W]
    ks_ref[0] = z[:, o + 2 * KV_W:o + 3 * KV_W].astype(BF16)
    vsT_ref[0] = z[:, o + 3 * KV_W:o + 4 * KV_W].T.astype(BF16)
    kw_ref[0] = z[:, o + 4 * KV_W:o + 5 * KV_W].astype(BF16)
    vwT_ref[0] = z[:, o + 5 * KV_W:o + 6 * KV_W].T.astype(BF16)
    o = o + 6 * KV_W
    cq = z[:, o:o + 256]
    ckv = z[:, o + 256:o + 384]
    misc = z[:, o + 384:o + 512]
    gT_ref[0] = jax.nn.sigmoid(misc).T
    cqn = (_rms(cq) * qng_ref[...]).astype(BF16)
    cos_q = jnp.tile(cq_ref[0], (1, MLA_HEADS))
    sin_q = jnp.tile(sq_ref[0], (1, MLA_HEADS))
    qm = _dot(cqn, wqa_ref[...]) * cos_q + _dot(cqn, wqb_ref[...]) * sin_q
    qmT_ref[0] = qm.T.astype(BF16)
    ckvn = (_rms(ckv) * kvng_ref[...]).astype(BF16)
    kr = (misc * ck_ref[0]).astype(BF16)
    km_ref[0] = (_dot(ckvn, wk_ref[...]) + _dot(kr, prope_ref[...])).astype(BF16)
    vmT_ref[0] = _dot(ckvn, wv_ref[...]).T.astype(BF16)


def _inproj(x, g1, win_p, qng, kvng, wqa, wqb, wk, wv, prope, cq_tab, sq_tab, ck_tab):
    B, S, D = x.shape
    tm = TM_IN
    full = lambda a: pl.BlockSpec(a.shape, lambda b, i: (0,) * a.ndim)
    tok = lambda w: pl.BlockSpec((1, tm, w), lambda b, i: (b, i, 0))
    tr = lambda w: pl.BlockSpec((1, w, tm), lambda b, i: (b, 0, i))
    outs = [
        (jax.ShapeDtypeStruct((B, NSA_W, S), BF16), tr(NSA_W)),
        (jax.ShapeDtypeStruct((B, S, KV_W), F32), tok(KV_W)),
        (jax.ShapeDtypeStruct((B, S, KV_W), F32), tok(KV_W)),
        (jax.ShapeDtypeStruct((B, S, KV_W), BF16), tok(KV_W)),
        (jax.ShapeDtypeStruct((B, S, KV_W), BF16), tok(KV_W)),
        (jax.ShapeDtypeStruct((B, KV_W, S), BF16), tr(KV_W)),
        (jax.ShapeDtypeStruct((B, KV_W, S), BF16), tr(KV_W)),
        (jax.ShapeDtypeStruct((B, LANE, S), F32), tr(LANE)),
        (jax.ShapeDtypeStruct((B, MLA_HEADS * MLA_QK, S), BF16), tr(MLA_HEADS * MLA_QK)),
        (jax.ShapeDtypeStruct((B, S, MLA_HEADS * MLA_QK), BF16), tok(MLA_HEADS * MLA_QK)),
        (jax.ShapeDtypeStruct((B, MLA_HEADS * V_DIM, S), BF16), tr(MLA_HEADS * V_DIM)),
    ]
    return pl.pallas_call(
        _inproj_kernel,
        grid=(B, S // tm),
        in_specs=[tok(D), full(g1), full(win_p), full(qng), full(kvng), full(wqa), full(wqb),
                  full(wk), full(wv), full(prope), tok(LANE), tok(LANE), tok(LANE)],
        out_specs=[o[1] for o in outs],
        out_shape=[o[0] for o in outs],
        compiler_params=_cparams("parallel", "parallel"),
        name="inproj",
    )(x, g1, win_p, qng, kvng, wqa, wqb, wk, wv, prope, cq_tab, sq_tab, ck_tab)


def _gelu(x):
    return 0.5 * x * (1.0 + lax.erf(x * (1.0 / math.sqrt(2.0))))


def _compress_kernel(c_ref, plo_ref, phi_ref, w1lo_ref, w1hi_ref, w2_ref, out_ref):
    c = c_ref[0, 0]
    nc = c.shape[0]
    a = _dot((c + plo_ref[...]).astype(BF16), w1lo_ref[0])
    b = _dot((c + phi_ref[...]).astype(BF16), w1hi_ref[0])
    hid = a + pltpu.roll(b, nc - 1, 0)
    out_ref[0, 0] = _dot(_gelu(hid).astype(BF16), w2_ref[0]).astype(BF16)


def _compress(c_all, pos_lo, pos_hi, w1lo, w1hi, w2):
    _, BG, nc, W = c_all.shape
    return pl.pallas_call(
        _compress_kernel,
        grid=(2, BG),
        in_specs=[pl.BlockSpec((1, 1, nc, W), lambda t, i: (t, i, 0, 0)),
                  pl.BlockSpec(pos_lo.shape, lambda t, i: (0, 0)),
                  pl.BlockSpec(pos_hi.shape, lambda t, i: (0, 0)),
                  pl.BlockSpec((1,) + w1lo.shape[1:], lambda t, i: (t, 0, 0)),
                  pl.BlockSpec((1,) + w1hi.shape[1:], lambda t, i: (t, 0, 0)),
                  pl.BlockSpec((1,) + w2.shape[1:], lambda t, i: (t, 0, 0))],
        out_specs=pl.BlockSpec((1, 1, nc, HEAD_DIM), lambda t, i: (t, i, 0, 0)),
        out_shape=jax.ShapeDtypeStruct((2, BG, nc, HEAD_DIM), BF16),
        compiler_params=_cparams("parallel", "parallel"),
        name="compress",
    )(c_all, pos_lo, pos_hi, w1lo, w1hi, w2)


def _nsa_cmp_kernel(qT_ref, kc_ref, vcT_ref, ft_ref, ovT_ref, ocT_ref, selb_ref):
    qi = pl.program_id(2)
    tq = qT_ref.shape[2]
    nc = kc_ref.shape[1]
    nsel = ovT_ref.shape[0]
    k = kc_ref[0]
    vT = vcT_ref[0]
    start = pl.multiple_of(nc - (tq // CMP_STRIDE) * qi, 8)
    psum = jnp.zeros((nc, tq), F32)
    scores = [_dot(k, qT_ref[0, r * HEAD_DIM:(r + 1) * HEAD_DIM, :]) for r in range(NSA_REP)]
    for r in range(NSA_REP):
        s = scores[r] + jnp.concatenate(
            [ft_ref[r, pl.ds(pl.multiple_of(start - (LANE // CMP_STRIDE) * a, 8), nc), :]
             for a in range(tq // LANE)], axis=1)
        m = jnp.max(s, axis=0, keepdims=True)
        p = jnp.exp2(s - m)
        l = jnp.sum(p, axis=0, keepdims=True)
        p = p * jnp.where(m > 0.5 * NEG, 1.0 / l, 0.0)
        ocT_ref[0, r * HEAD_DIM:(r + 1) * HEAD_DIM, :] = _dot(vT, p.astype(BF16))
        psum = psum + p
    p_hi = psum.astype(BF16)
    p_lo = (psum - p_hi.astype(F32)).astype(BF16)
    imp = _dot(ovT_ref[...], p_hi) + _dot(ovT_ref[...], p_lo)

    sid = lax.broadcasted_iota(jnp.int32, (nsel, tq), 0)
    t = qi * tq + lax.broadcasted_iota(jnp.int32, (nsel, tq), 1)
    cur = jnp.right_shift(t, int(math.log2(SEL_LEN)))
    valid = sid <= cur
    forced = (sid == 0) | (sid == cur) | (sid == cur - 1)
    n_free = SEL_TOPN - 1 - (cur[0:1] >= 1).astype(jnp.int32) - (cur[0:1] >= 2).astype(jnp.int32)
    work = jnp.where(valid & jnp.logical_not(forced), imp, NEG)
    tau = jnp.zeros((1, tq), F32)
    for it in range(1, SEL_TOPN):
        mx = jnp.max(work, axis=0, keepdims=True)
        if it >= SEL_TOPN - 3:
            tau = jnp.where(n_free == it, mx, tau)
        if it < SEL_TOPN - 1:
            work = jnp.where(work >= mx, REMOVED, work)
    sel = valid & (forced | (imp >= tau))
    selb_ref[0, 0] = jnp.where(sel, 0.0, NEG)


def _nsa_cmp(qT, kcmp, vcmpT, ft, ovT):
    B, _, S = qT.shape
    G = NSA_KV_HEADS
    nc = kcmp.shape[1]
    nsel = ovT.shape[0]
    tq = TQ_CMP
    gw = NSA_REP * HEAD_DIM
    return pl.pallas_call(
        _nsa_cmp_kernel,
        grid=(B, G, S // tq),
        in_specs=[pl.BlockSpec((1, gw, tq), lambda b, g, i: (b, g, i)),
                  pl.BlockSpec((1, nc, HEAD_DIM), lambda b, g, i: (b * G + g, 0, 0)),
                  pl.BlockSpec((1, HEAD_DIM, nc), lambda b, g, i: (b * G + g, 0, 0)),
                  pl.BlockSpec((NSA_REP, 2 * nc, LANE), lambda b, g, i: (g, 0, 0)),
                  pl.BlockSpec(ovT.shape, lambda b, g, i: (0, 0))],
        out_specs=[pl.BlockSpec((1, gw, tq), lambda b, g, i: (b, g, i)),
                   pl.BlockSpec((1, 1, nsel, tq), lambda b, g, i: (b, g, 0, i))],
        out_shape=[jax.ShapeDtypeStruct((B, NSA_W, S), F32),
                   jax.ShapeDtypeStruct((B, G, nsel, S), F32)],
        compiler_params=_cparams("parallel", "parallel", "parallel"),
        name="nsa_cmp",
    )(qT, kcmp, vcmpT, ft, ovT)


def _sub_all(x8, op):
    for sh in (4, 2, 1):
        x8 = op(x8, pltpu.roll(x8, sh, 0))
    return x8


def _col_max8(s):
    return jnp.max(s.reshape(s.shape[0] // 8, 8, s.shape[1]), axis=0)


def _fa_init(m_ref, al_ref, acc_ref):
    m_ref[...] = jnp.full(m_ref.shape, REMOVED, F32)
    al_ref[...] = jnp.ones(al_ref.shape, F32)
    acc_ref[...] = jnp.zeros(acc_ref.shape, F32)


def _fa_update_m(st, idx, mx8):
    m_ref, al_ref = st[0], st[1]
    m_old = m_ref[idx]
    m_new = jnp.maximum(m_old, _sub_all(mx8, jnp.maximum))
    al_ref[idx] = jnp.exp2(m_old - m_new)
    m_ref[idx] = m_new


def _fa_pass2(s_buf, vT, st, idx):
    m_ref, al_ref, acc_ref = st
    tk, tq = s_buf.shape
    dv = acc_ref.shape[1]
    p3 = jnp.exp2(s_buf[...].reshape(tk // 8, 8, tq) - m_ref[idx][None])
    pv = _dot(vT, p3.reshape(tk, tq).astype(BF16))
    acc_ref[idx] = (acc_ref[idx].reshape(dv // 8, 8, tq) * al_ref[idx][None]).reshape(dv, tq) + pv


def _fa_result(st, idx, dv):
    acc = st[2][idx]
    tq = acc.shape[1]
    inv = 1.0 / acc[dv:dv + 8]
    return (acc[:dv].reshape(dv // 8, 8, tq) * inv[None]).reshape(dv, tq)


def _with_ones_rows(vT5):
    return jnp.concatenate([vT5, jnp.ones(vT5.shape[:-2] + (V_ONES, vT5.shape[-1]), vT5.dtype)], axis=-2)


def _fa_segment(c0, n, pass1, pass2, update):
    last = c0 + n - 1
    update(pass1(c0, 0))

    def body(i, carry):
        c = c0 + 2 * i
        mx = pass1(c + 1, 1)
        pass2(c, 0)
        update(mx)
        mx = pass1(jnp.where(c + 2 <= last, c + 2, c), 0)
        pass2(c + 1, 1)
        update(mx)
        return carry

    lax.fori_loop(0, n // 2, body, 0)


def _nsa_sel_kernel(qT_ref, ks_ref, vsT_ref, selb_ref, ts_ref, kw_ref, vwT_ref, tw_ref, osT_ref, owT_ref,
                    s_ref, m_ref, al_ref, acc_ref):
    qi = pl.program_id(2)
    tq = qT_ref.shape[2]
    st = (m_ref, al_ref, acc_ref)
    blocks = TK // SEL_LEN

    def table_bias(t_ref, r, base):
        return jnp.concatenate([t_ref[r, pl.ds(pl.multiple_of(base - LANE * a, LANE), TK), :]
                                for a in range(tq // LANE)], axis=1)

    def make_pass1(near):
        def pass1(c, slot):
            k = ks_ref[0, 0, pl.ds(pl.multiple_of(c * TK, TK), TK), :]
            mb = selb_ref[0, 0, pl.ds(pl.multiple_of((c // 2) * 2 * blocks, 2 * blocks), 2 * blocks), :]
            mb = mb[slot * blocks:(slot + 1) * blocks]
            out = []
            for r in range(NSA_REP):
                s = _dot(k, qT_ref[0, r * HEAD_DIM:(r + 1) * HEAD_DIM, :])
                if near:
                    s = s + table_bias(ts_ref, r, SEL_ROFF - TK * (2 * qi - c))
                s = jnp.concatenate(
                    [s[j * SEL_LEN:(j + 1) * SEL_LEN] + mb[j:j + 1] for j in range(blocks)], axis=0)
                s_ref[slot, r] = s
                out.append(_col_max8(s))
            return out
        return pass1

    def win_pass1(c, slot):
        k = kw_ref[0, 0, pl.ds(pl.multiple_of(c * TK, TK), TK), :]
        out = []
        for r in range(NSA_REP):
            s = _dot(k, qT_ref[0, r * HEAD_DIM:(r + 1) * HEAD_DIM, :])
            s = s + table_bias(tw_ref, r, WIN_ROFF2 - TK * (2 * qi - c))
            s_ref[slot, r] = s
            out.append(_col_max8(s))
        return out

    def make_pass2(vT_ref):
        def pass2(c, slot):
            vT = vT_ref[0, 0, c]
            for r in range(NSA_REP):
                _fa_pass2(s_ref.at[slot, r], vT, st, r)
        return pass2

    def update(mx):
        for r in range(NSA_REP):
            _fa_update_m(st, r, mx[r])

    _fa_init(*st)
    n_far = jnp.maximum(0, 2 * qi - SEL_WMAX)

    @pl.when(n_far > 0)
    def _():
        _fa_segment(0, n_far, make_pass1(False), make_pass2(vsT_ref), update)

    _fa_segment(n_far, 2 * qi + 2 - n_far, make_pass1(True), make_pass2(vsT_ref), update)
    for r in range(NSA_REP):
        osT_ref[0, r * HEAD_DIM:(r + 1) * HEAD_DIM, :] = _fa_result(st, r, HEAD_DIM)

    _fa_init(*st)
    w0 = jnp.maximum(0, 2 * qi - WIN_CHUNKS_BACK)
    _fa_segment(w0, 2 * qi + 2 - w0, win_pass1, make_pass2(vwT_ref), update)
    for r in range(NSA_REP):
        owT_ref[0, r * HEAD_DIM:(r + 1) * HEAD_DIM, :] = _fa_result(st, r, HEAD_DIM)


def _nsa_sel(qT, ks4, vsT5, selb, ts, kw4, vwT5, tw):
    B, _, S = qT.shape
    G = NSA_KV_HEADS
    nsel = selb.shape[2]
    gw = NSA_REP * HEAD_DIM
    kspec = pl.BlockSpec((1, 1, S, HEAD_DIM), lambda b, g, i: (b, g, 0, 0))
    vspec = pl.BlockSpec((1, 1, S // TK, HEAD_DIM + V_ONES, TK), lambda b, g, i: (b, g, 0, 0, 0))
    tspec = lambda t: pl.BlockSpec((NSA_REP,) + t.shape[1:], lambda b, g, i: (g, 0, 0),
                                   pipeline_mode=pl.Buffered(1))
    ospec = pl.BlockSpec((1, gw, TQ), lambda b, g, i: (b, g, i))
    return pl.pallas_call(
        _nsa_sel_kernel,
        grid=(B, G, S // TQ),
        in_specs=[pl.BlockSpec((1, gw, TQ), lambda b, g, i: (b, g, i)), kspec, vspec,
                  pl.BlockSpec((1, 1, nsel, TQ), lambda b, g, i: (b, g, 0, i)), tspec(ts),
                  kspec, vspec, tspec(tw)],
        out_specs=[ospec, ospec],
        out_shape=[jax.ShapeDtypeStruct((B, NSA_W, S), F32)] * 2,
        scratch_shapes=[pltpu.VMEM((2, NSA_REP, TK, TQ), F32)]
        + [pltpu.VMEM((NSA_REP, 8, TQ), F32)] * 2 + [pltpu.VMEM((NSA_REP, HEAD_DIM + V_ONES, TQ), F32)],
        compiler_params=_cparams("parallel", "parallel", "parallel"),
        name="nsa_sel_win",
    )(qT, ks4, vsT5, selb, ts, kw4, vwT5, tw)


MLA_HPS = 2


def _mla_kernel(qT_ref, k_ref, vT_ref, oT_ref, s_ref, m_ref, al_ref, acc_ref):
    qi = pl.program_id(2)
    tq = qT_ref.shape[2]
    st = (m_ref, al_ref, acc_ref)
    _fa_init(*st)

    def make_pass1(masked):
        def pass1(c, slot):
            out = []
            for h in range(MLA_HPS):
                k = k_ref[0, pl.ds(pl.multiple_of(c * TK, TK), TK), h * MLA_QK:(h + 1) * MLA_QK]
                s = _dot(k, qT_ref[0, h * MLA_QK:(h + 1) * MLA_QK, :])
                if masked:
                    kpos = c * TK + lax.broadcasted_iota(jnp.int32, (TK, tq), 0)
                    qpos = qi * TQ + lax.broadcasted_iota(jnp.int32, (TK, tq), 1)
                    s = jnp.where(kpos <= qpos, s, NEG)
                s_ref[slot, h] = s
                out.append(_col_max8(s))
            return out
        return pass1

    def pass2(c, slot):
        for h in range(MLA_HPS):
            _fa_pass2(s_ref.at[slot, h], vT_ref[0, h, c], st, h)

    def update(mx):
        for h in range(MLA_HPS):
            _fa_update_m(st, h, mx[h])

    @pl.when(qi > 0)
    def _():
        _fa_segment(0, 2 * qi, make_pass1(False), pass2, update)

    diag = make_pass1(True)
    update(diag(2 * qi, 0))
    mx = diag(2 * qi + 1, 1)
    pass2(2 * qi, 0)
    update(mx)
    pass2(2 * qi + 1, 1)
    for h in range(MLA_HPS):
        oT_ref[0, h * V_DIM:(h + 1) * V_DIM, :] = _fa_result(st, h, V_DIM)


def _mla(qmT, km, vmT5):
    B, _, S = qmT.shape
    hp = MLA_HPS
    return pl.pallas_call(
        _mla_kernel,
        grid=(B, MLA_HEADS // hp, S // TQ),
        in_specs=[pl.BlockSpec((1, hp * MLA_QK, TQ), lambda b, h, i: (b, h, i)),
                  pl.BlockSpec((1, S, hp * MLA_QK), lambda b, h, i: (b, 0, h)),
                  pl.BlockSpec((1, hp, S // TK, V_DIM + V_ONES, TK), lambda b, h, i: (b, h, 0, 0, 0))],
        out_specs=pl.BlockSpec((1, hp * V_DIM, TQ), lambda b, h, i: (b, h, i)),
        out_shape=jax.ShapeDtypeStruct((B, MLA_HEADS * V_DIM, S), F32),
        scratch_shapes=[pltpu.VMEM((2, hp, TK, TQ), F32)] + [pltpu.VMEM((hp, 8, TQ), F32)] * 2
        + [pltpu.VMEM((hp, V_DIM + V_ONES, TQ), F32)],
        compiler_params=_cparams("parallel", "parallel", "parallel"),
        name="mla",
    )(qmT, km, vmT5)


def _outproj_kernel(ocT_ref, osT_ref, owT_ref, omT_ref, gT_ref, x_ref, gn_ref, gm_ref, wout_ref,
                    g2_ref, x2_ref, hnT_ref):
    parts = []
    for h in range(NSA_HEADS):
        rows = slice(h * HEAD_DIM, (h + 1) * HEAD_DIM)
        parts.append(gT_ref[0, 3 * h:3 * h + 1, :] * ocT_ref[0, rows, :]
                     + gT_ref[0, 3 * h + 1:3 * h + 2, :] * osT_ref[0, rows, :]
                     + gT_ref[0, 3 * h + 2:3 * h + 3, :] * owT_ref[0, rows, :])
    nsaT = jnp.concatenate(parts, axis=0)

    def norm_t(yT, g):
        y = yT * lax.rsqrt(jnp.mean(yT * yT, axis=0, keepdims=True) + EPS)
        return (y.T * g).astype(BF16)

    y_nsa = norm_t(nsaT, gn_ref[...])
    y_mla = norm_t(omT_ref[0], gm_ref[...])
    x2 = x_ref[0] + _dot(y_nsa, wout_ref[:NSA_W, :]) + _dot(y_mla, wout_ref[NSA_W:, :])
    x2_ref[0] = x2
    hnT_ref[...] = (_rms(x2) * g2_ref[...]).T.astype(BF16)


def _outproj(ocT, osT, owT, omT, gT, x, gn, gm, wout, g2):
    B, S, D = x.shape
    tm = TM_OUT
    nt = S // tm
    tr = lambda a: pl.BlockSpec((1, a.shape[1], tm), lambda b, i: (b, 0, i))
    full = lambda a: pl.BlockSpec(a.shape, lambda b, i: (0,) * a.ndim)
    return pl.pallas_call(
        _outproj_kernel,
        grid=(B, nt),
        in_specs=[tr(ocT), tr(osT), tr(owT), tr(omT), tr(gT),
                  pl.BlockSpec((1, tm, D), lambda b, i: (b, i, 0)),
                  full(gn), full(gm), full(wout), full(g2)],
        out_specs=[pl.BlockSpec((1, tm, D), lambda b, i: (b, i, 0)),
                   pl.BlockSpec((D, tm), lambda b, i: (0, b * nt + i))],
        out_shape=[jax.ShapeDtypeStruct((B, S, D), F32),
                   jax.ShapeDtypeStruct((D, B * S), BF16)],
        compiler_params=_cparams("parallel", "parallel"),
        name="outproj",
    )(ocT, osT, owT, omT, gT, x, gn, gm, wout, g2)


def _row_max_bcast(w3):
    m8 = jnp.max(w3, axis=0)
    for sh in (4, 2, 1):
        m8 = jnp.maximum(m8, pltpu.roll(m8, sh, 0))
    return m8


def _top_values(s, n):
    w3 = s.reshape(s.shape[0] // 8, 8, s.shape[1])
    vals = []
    for it in range(n):
        mx = _row_max_bcast(w3)
        vals.append(mx)
        if it < n - 1:
            w3 = jnp.where(w3 >= mx[None], REMOVED, w3)
    return vals


def _pair_list(n):
    return [(i, j) for i in range(n) for j in range(n) if (i + 1) * (j + 1) <= n]


def _top_pair_sums(a, b, n):
    cands = [a[i] + b[j] for i, j in _pair_list(n)]
    vals = []
    for it in range(n):
        mx = functools.reduce(jnp.maximum, cands)
        vals.append(mx)
        if it < n - 1:
            cands = [jnp.where(c >= mx, REMOVED, c) for c in cands]
    return vals


def _peer_topk_kernel(hnT_ref, wqT_ref, k1_ref, k2_ref, c_ref, s2_ref, e1_ref, b1_ref):
    tm = hnT_ref.shape[1]
    n = PEER_TOPK + 1
    qT = _dot(wqT_ref[...], hnT_ref[...]).astype(BF16)
    dk = k1_ref.shape[1]
    sub = lax.broadcasted_iota(jnp.int32, (8, tm), 0)
    a_m = [jnp.zeros((8, tm), F32)] * n
    b_m = [jnp.zeros((8, tm), F32)] * n
    for h in range(PEER_HEADS):
        s1 = _dot(k1_ref[...], qT[2 * h * dk:(2 * h + 1) * dk])
        s2 = _dot(k2_ref[...], qT[(2 * h + 1) * dk:(2 * h + 2) * dk])
        c_ref[h] = s1
        s2_ref[h] = s2
        a = _top_values(s1, n)
        b = _top_values(s2, n)
        a_m = [jnp.where(sub == h, a[i], a_m[i]) for i in range(n)]
        b_m = [jnp.where(sub == h, b[i], b_m[i]) for i in range(n)]
    v = _top_pair_sums(a_m, b_m, n)
    z = functools.reduce(jnp.add, [jnp.exp(v[i] - v[0]) for i in range(PEER_TOPK)])
    thr = 0.5 * (v[PEER_TOPK - 1] + v[PEER_TOPK])
    for h in range(PEER_HEADS):
        c_ref[h] = thr[h:h + 1, :] - c_ref[h]
    e1_ref[...] = thr - a_m[0] - jnp.log(z)
    b1_ref[...] = b_m[0]


def _peer_topk(hnT, wqT, k1, k2):
    D, N = hnT.shape
    tm = TM_TOPK
    H = PEER_HEADS
    nk = k1.shape[0]
    full = lambda a: pl.BlockSpec(a.shape, lambda i: (0,) * a.ndim)
    return pl.pallas_call(
        _peer_topk_kernel,
        grid=(N // tm,),
        in_specs=[pl.BlockSpec((D, tm), lambda i: (0, i)), full(wqT), full(k1), full(k2)],
        out_specs=[pl.BlockSpec((H, nk, tm), lambda i: (0, 0, i)),
                   pl.BlockSpec((H, nk, tm), lambda i: (0, 0, i)),
                   pl.BlockSpec((H, tm), lambda i: (0, i)),
                   pl.BlockSpec((H, tm), lambda i: (0, i))],
        out_shape=[jax.ShapeDtypeStruct((H, nk, N), F32),
                   jax.ShapeDtypeStruct((H, nk, N), F32),
                   jax.ShapeDtypeStruct((H, N), F32),
                   jax.ShapeDtypeStruct((H, N), F32)],
        compiler_params=_cparams("parallel"),
        name="peer_topk",
    )(hnT, wqT, k1, k2)


def _peer_dense_kernel(hnT_ref, u_ref, vT_ref, c_ref, s2_ref, e1_ref, b1_ref, x2_ref, gf_ref, out_ref,
                       acc_ref, p1_ref, p2_ref):
    j = pl.program_id(1)
    nk = s2_ref.shape[1]
    ne = u_ref.shape[0]
    tm = hnT_ref.shape[1]

    @pl.when(j == 0)
    def _():
        acc_ref[...] = jnp.zeros(acc_ref.shape, F32)
        for h in range(PEER_HEADS):
            p1_ref[h] = jnp.exp(e1_ref[h:h + 1, :] - c_ref[h])
            p2_ref[h] = jnp.exp(s2_ref[h] - b1_ref[h:h + 1, :])

    act = _dot(u_ref[...], hnT_ref[...])
    ws = []
    for kk in range(ne // nk):
        i1 = j * (ne // nk) + kk
        gate = jnp.zeros((nk, tm), F32)
        for h in range(PEER_HEADS):
            sel = s2_ref[h] >= c_ref[h, pl.ds(i1, 1), :]
            gate = gate + jnp.where(sel, p1_ref[h, pl.ds(i1, 1), :] * p2_ref[h], 0.0)
        ws.append((gate * _gelu(act[kk * nk:(kk + 1) * nk])).astype(BF16))
    acc_ref[...] += _dot(vT_ref[...], jnp.concatenate(ws, axis=0))

    @pl.when(j == pl.num_programs(1) - 1)
    def _():
        y = x2_ref[...] + acc_ref[...].T
        out_ref[...] = _rms(y) * gf_ref[...]


def _peer_dense(hnT, u, vT, c, s2, e1, b1, x2, gf):
    D, N = hnT.shape
    NE = u.shape[0]
    tm, ne = TM_PEER, NE_PEER
    H, nk, _ = s2.shape
    return pl.pallas_call(
        _peer_dense_kernel,
        grid=(N // tm, NE // ne),
        in_specs=[pl.BlockSpec((D, tm), lambda i, j: (0, i)),
                  pl.BlockSpec((ne, D), lambda i, j: (j, 0)),
                  pl.BlockSpec((D, ne), lambda i, j: (0, j)),
                  pl.BlockSpec((H, nk, tm), lambda i, j: (0, 0, i)),
                  pl.BlockSpec((H, nk, tm), lambda i, j: (0, 0, i)),
                  pl.BlockSpec((H, tm), lambda i, j: (0, i)),
                  pl.BlockSpec((H, tm), lambda i, j: (0, i)),
                  pl.BlockSpec((tm, D), lambda i, j: (i, 0)),
                  pl.BlockSpec(gf.shape, lambda i, j: (0, 0))],
        out_specs=pl.BlockSpec((tm, D), lambda i, j: (i, 0)),
        out_shape=jax.ShapeDtypeStruct((N, D), F32),
        scratch_shapes=[pltpu.VMEM((D, tm), F32), pltpu.VMEM((H, nk, tm), F32),
                        pltpu.VMEM((H, nk, tm), F32)],
        compiler_params=_cparams("parallel", "arbitrary"),
        name="peer_dense",
    )(hnT, u, vT, c, s2, e1, b1, x2, gf)


def _t5_bucket(dist):
    dist = jnp.maximum(dist, 0)
    max_exact = N_BUCKETS // 2
    d = jnp.maximum(dist, 1).astype(F32)
    large = max_exact + (jnp.log(d / max_exact) / math.log(MAX_DISTANCE / max_exact)
                         * (N_BUCKETS - max_exact)).astype(jnp.int32)
    return jnp.where(dist < max_exact, dist, jnp.minimum(large, N_BUCKETS - 1))


def _toeplitz(f, rows):
    H = f.shape[0]
    nb = rows // LANE
    f = jnp.pad(f, ((0, 0), (0, (nb + 1) * LANE - f.shape[1])))
    seg = f.reshape(H, nb + 1, LANE)
    seg = jnp.concatenate([seg[:, :-1], seg[:, 1:]], axis=-1)[:, ::-1]
    seg = jnp.roll(seg, -(LANE - 1), axis=-1)
    toep = jnp.tile(seg, (1, 1, LANE))[..., :LANE * (2 * LANE - 1)]
    toep = toep.reshape(H, nb, LANE, 2 * LANE - 1)[..., :LANE]
    return toep.reshape(H, rows, LANE)


def _bias_tables(t5_bias, nc):
    onehot = lambda d: (_t5_bucket(d)[:, None] == jnp.arange(N_BUCKETS)[None, :]).astype(F32)

    def rel(d):
        return jnp.einsum("nb,bh->hn", onehot(d), t5_bias.astype(F32), precision=lax.Precision.HIGHEST)

    far = t5_bias[N_BUCKETS - 1].astype(F32)[:, None]
    x = jnp.arange(CMP_STRIDE * (2 * nc - 1) + LANE)
    d = x - CMP_STRIDE * (nc - 1) - (CMP_LEN - 1)
    f16 = jnp.where(d >= 0, rel(d) * LOG2E, NEG).reshape(t5_bias.shape[1], -1, CMP_STRIDE)[:, ::-1]
    nk16 = LANE // CMP_STRIDE
    ft = jnp.concatenate([f16[:, nk16 - 1 - k:nk16 - 1 - k + 2 * nc] for k in range(nk16)], axis=-1)
    d = jnp.arange(SEL_ROWS + LANE - 1) - (SEL_ROWS - 1) + SEL_ROFF
    ts = _toeplitz(jnp.where(d >= 0, (rel(d) - far) * LOG2E, NEG), SEL_ROWS)
    d = jnp.arange(WIN_ROWS2 + LANE - 1) - (WIN_ROWS2 - 1) + WIN_ROFF2
    tw = _toeplitz(jnp.where((d >= 0) & (d < WINDOW), rel(d) * LOG2E, NEG), WIN_ROWS2)
    return ft, ts, tw


def _overlap_t(nc, n_cmp, nsel):
    cs = np.arange(nc) * CMP_STRIDE
    ce = cs + CMP_LEN - 1
    ss = np.arange(nsel) * SEL_LEN
    ov = (cs[None, :] < ss[:, None] + SEL_LEN) & (ce[None, :] >= ss[:, None]) & (np.arange(nc)[None, :] < n_cmp)
    return jnp.asarray(ov, BF16)


def kernel(x, positions, norm1_g, w_in, cmp_pos, cmp_k_w1, cmp_k_w2, cmp_v_w1, cmp_v_w2, t5_bias,
           q_norm_g, w_q_up, kv_norm_g, w_kv_up, grp_norm_nsa, grp_norm_mla, w_out, norm2_g,
           peer_wq, peer_keys1, peer_keys2, peer_u, peer_v, final_g):
    B, S, D = x.shape
    G, H = NSA_KV_HEADS, MLA_HEADS
    assert S % TQ == 0 and S % TM_IN == 0 and norm1_g.shape[0] == 1
    nc = S // CMP_STRIDE
    n_cmp = (S - CMP_LEN) // CMP_STRIDE + 1
    nsel = S // SEL_LEN
    half = QK_ROPE // 2

    w = w_in[0]
    o_g = NSA_W + 6 * KV_W
    n_gate = 3 * NSA_HEADS
    o_cq = o_g + n_gate
    q_lora = w_q_up.shape[1]
    kv_lora = w_kv_up.shape[1]
    o_ckv = o_cq + q_lora
    o_kr = o_ckv + kv_lora
    kr1, kr2 = w[:, o_kr:o_kr + half], w[:, o_kr + half:o_kr + 2 * half]
    misc = jnp.concatenate([kr1, kr2, -kr2, kr1, w[:, o_g:o_g + n_gate],
                            jnp.zeros((D, LANE - 4 * half - n_gate), F32)], axis=1)
    win_p = jnp.concatenate([w[:, :o_g], w[:, o_cq:o_kr], misc], axis=1).astype(BF16)

    wq = w_q_up[0].reshape(q_lora, H, QK_NOPE + QK_ROPE)
    q1, q2 = wq[..., QK_NOPE:QK_NOPE + half], wq[..., QK_NOPE + half:]
    zpad = jnp.zeros((q_lora, H, MLA_QK - QK_NOPE - QK_ROPE), F32)
    wqa = jnp.concatenate([wq[..., :QK_NOPE], q1, q2, zpad], -1).reshape(q_lora, H * MLA_QK).astype(BF16)
    wqb = jnp.concatenate([jnp.zeros_like(wq[..., :QK_NOPE]), -q2, q1, zpad], -1).reshape(
        q_lora, H * MLA_QK).astype(BF16)
    wkv = w_kv_up[0].reshape(kv_lora, H, QK_NOPE + V_DIM)
    wk = jnp.concatenate([wkv[..., :QK_NOPE], jnp.zeros((kv_lora, H, MLA_QK - QK_NOPE), F32)],
                         -1).reshape(kv_lora, H * MLA_QK).astype(BF16)
    wv = wkv[..., QK_NOPE:].reshape(kv_lora, H * V_DIM).astype(BF16)
    pr = np.zeros((LANE, H, MLA_QK), np.float32)
    for j in range(QK_ROPE):
        pr[j, :, QK_NOPE + j] = 1.0
        pr[QK_ROPE + j, :, QK_NOPE + j] = 1.0
    prope = jnp.asarray(pr.reshape(LANE, H * MLA_QK), BF16)

    inv = ROPE_THETA ** (-jnp.arange(half, dtype=F32) / half)
    ang = positions.astype(F32)[..., None] * inv
    cos, sin = jnp.cos(ang), jnp.sin(ang)
    one = jnp.ones((B, S, QK_NOPE), F32)
    zq = jnp.zeros((B, S, MLA_QK - QK_NOPE - QK_ROPE), F32)
    mla_scale = (QK_NOPE + QK_ROPE) ** -0.5 * LOG2E
    cq_tab = jnp.concatenate([one, cos, cos, zq], -1) * mla_scale
    sq_tab = jnp.concatenate([jnp.zeros_like(one), sin, sin, zq], -1) * mla_scale
    ck_tab = jnp.concatenate([cos, cos, sin, sin, jnp.zeros((B, S, LANE - 4 * half), F32)], -1)

    row = lambda v: v.reshape(1, -1).astype(F32)
    (qT, kc, vc, ks, kw, vsT, vwT, gmT, qmT, km, vmT) = _inproj(
        x, row(norm1_g[0]), win_p, row(q_norm_g[0]), row(kv_norm_g[0]), wqa, wqb, wk, wv, prope,
        cq_tab, sq_tab, ck_tab)

    def chunks16(a):
        return a.reshape(B, nc, CMP_STRIDE, G, HEAD_DIM).transpose(0, 3, 1, 2, 4).reshape(
            B * G, nc, CMP_STRIDE * HEAD_DIM)

    def heads_first(a):
        return a.reshape(B, S, G, HEAD_DIM).transpose(0, 2, 1, 3)

    def lane_chunks(aT, nh, dh, ck):
        L = aT.shape[-1]
        return aT.reshape(B, nh, dh, L // ck, ck).transpose(0, 1, 3, 2, 4)

    c_all = jnp.stack([chunks16(kc), chunks16(vc)])
    pos_lo = cmp_pos[0, :CMP_STRIDE].reshape(1, -1)
    pos_hi = cmp_pos[0, CMP_STRIDE:].reshape(1, -1)
    split = CMP_STRIDE * HEAD_DIM
    w1 = jnp.stack([cmp_k_w1[0], cmp_v_w1[0]]).astype(BF16)
    w2 = jnp.stack([cmp_k_w2[0], cmp_v_w2[0]]).astype(BF16)
    cmp_out = _compress(c_all, pos_lo, pos_hi, w1[:, :split], w1[:, split:], w2)
    kcmp = cmp_out[0]
    vcmpT = cmp_out[1].transpose(0, 2, 1)

    ft, ts, tw = _bias_tables(t5_bias, nc)
    ocT, selb = _nsa_cmp(qT, kcmp, vcmpT, ft, _overlap_t(nc, n_cmp, nsel))
    osT, owT = _nsa_sel(qT, heads_first(ks), _with_ones_rows(lane_chunks(vsT, G, HEAD_DIM, TK)), selb, ts,
                        heads_first(kw), _with_ones_rows(lane_chunks(vwT, G, HEAD_DIM, TK)), tw)
    omT = _mla(qmT, km, _with_ones_rows(lane_chunks(vmT, H, V_DIM, TK)))

    o_gate = 4 * half
    x2, hnT = _outproj(ocT, osT, owT, omT, gmT[:, o_gate:o_gate + n_gate], x, row(grp_norm_nsa[0]),
                       row(grp_norm_mla[0]), w_out[0].astype(BF16), row(norm2_g[0]))

    c, s2, e1, b1 = _peer_topk(hnT, peer_wq[0].T.astype(BF16), peer_keys1[0].astype(BF16),
                               peer_keys2[0].astype(BF16))
    out = _peer_dense(hnT, peer_u[0].astype(BF16), peer_v[0].T.astype(BF16), c, s2, e1, b1,
                      x2.reshape(B * S, D), row(final_g))
    return out.reshape(B, S, D)
```

```python
import functools
import math

import numpy as np
import jax
import jax.numpy as jnp
from jax import lax
from jax.experimental import pallas as pl
from jax.experimental.pallas import tpu as pltpu

F32 = jnp.float32
BF16 = jnp.bfloat16

HEAD_DIM = 64
NSA_HEADS = 8
NSA_KV_HEADS = 2
NSA_REP = NSA_HEADS // NSA_KV_HEADS
CMP_LEN = 32
CMP_STRIDE = 16
SEL_LEN = 64
SEL_TOPN = 16
WINDOW = 512
SEL_FORCE = 1e4
MLA_HEADS = 8
QK_NOPE = 64
QK_ROPE = 32
V_DIM = 64
ROPE_THETA = 10000.0
N_BUCKETS = 32
MAX_DISTANCE = 2048
PEER_HEADS = 8
PEER_TOPK = 16
EPS = 1e-6
NEG = -1e30
REMOVED = -3e38

LANE = 128
NSA_W = NSA_HEADS * HEAD_DIM
KV_W = NSA_KV_HEADS * HEAD_DIM
MLA_QK = 128
FAR_DIST = 2048

TM_IN = 512
TQ_CMP = 256
TQ = 512
TK = 256
TM_OUT = 256
TM_TOPK = 256
TM_PEER = 512
NE_PEER = 512
SEL_WMAX = (FAR_DIST + TK - 1) // TK
SEL_ROFF = TK * SEL_WMAX + TQ - LANE
SEL_ROWS = SEL_ROFF + 2 * TK
WIN_CHUNKS_BACK = WINDOW // TK
WIN_ROFF2 = TK * WIN_CHUNKS_BACK + TQ - LANE
WIN_ROWS2 = WIN_ROFF2 + 2 * TK
LOG2E = math.log2(math.e)
V_ONES = 16
VMEM_LIMIT = 56 * 1024 * 1024


def _cparams(*sem):
    return pltpu.CompilerParams(dimension_semantics=sem, vmem_limit_bytes=VMEM_LIMIT)


def _rms(x):
    return x * lax.rsqrt(jnp.mean(x * x, axis=-1, keepdims=True) + EPS)


def _dot(a, b):
    return jnp.dot(a, b, preferred_element_type=F32)


def _inproj_kernel(x_ref, g1_ref, win_ref, qng_ref, kvng_ref, wqa_ref, wqb_ref, wk_ref, wv_ref,
                   prope_ref, cq_ref, sq_ref, ck_ref,
                   qT_ref, kc_ref, vc_ref, ks_ref, kw_ref, vsT_ref, vwT_ref, gT_ref,
                   qmT_ref, km_ref, vmT_ref):
    x = x_ref[0]
    h = _rms(x) * g1_ref[...]
    z = _dot(h.astype(BF16), win_ref[...])
    qT_ref[0] = (z[:, :NSA_W] * (HEAD_DIM ** -0.5 * LOG2E)).T.astype(BF16)
    o = NSA_W
    kc_ref[0] = z[:, o:o + KV_W]
    vc_ref[0] = z[:, o + KV_W:o + 2 * KV_W]
    ks_ref[0] = z[:, o + 2 * KV_W:o + 3 * KV_W].astype(BF16)
    vsT_ref[0] = z[:, o + 3 * KV_W:o + 4 * KV_W].T.astype(BF16)
    kw_ref[0] = z[:, o + 4 * KV_W:o + 5 * KV_W].astype(BF16)
    vwT_ref[0] = z[:, o + 5 * KV_W:o + 6 * KV_W].T.astype(BF16)
    o = o + 6 * KV_W
    cq = z[:, o:o + 256]
    ckv = z[:, o + 256:o + 384]
    misc = z[:, o + 384:o + 512]
    gT_ref[0] = jax.nn.sigmoid(misc).T
    cqn = (_rms(cq) * qng_ref[...]).astype(BF16)
    cos_q = jnp.tile(cq_ref[0], (1, MLA_HEADS))
    sin_q = jnp.tile(sq_ref[0], (1, MLA_HEADS))
    qm = _dot(cqn, wqa_ref[...]) * cos_q + _dot(cqn, wqb_ref[...]) * sin_q
    qmT_ref[0] = qm.T.astype(BF16)
    ckvn = (_rms(ckv) * kvng_ref[...]).astype(BF16)
    kr = (misc * ck_ref[0]).astype(BF16)
    km_ref[0] = (_dot(ckvn, wk_ref[...]) + _dot(kr, prope_ref[...])).astype(BF16)
    vmT_ref[0] = _dot(ckvn, wv_ref[...]).T.astype(BF16)


def _inproj(x, g1, win_p, qng, kvng, wqa, wqb, wk, wv, prope, cq_tab, sq_tab, ck_tab):
    B, S, D = x.shape
    tm = TM_IN
    full = lambda a: pl.BlockSpec(a.shape, lambda b, i: (0,) * a.ndim)
    tok = lambda w: pl.BlockSpec((1, tm, w), lambda b, i: (b, i, 0))
    tr = lambda w: pl.BlockSpec((1, w, tm), lambda b, i: (b, 0, i))
    outs = [
        (jax.ShapeDtypeStruct((B, NSA_W, S), BF16), tr(NSA_W)),
        (jax.ShapeDtypeStruct((B, S, KV_W), F32), tok(KV_W)),
        (jax.ShapeDtypeStruct((B, S, KV_W), F32), tok(KV_W)),
        (jax.ShapeDtypeStruct((B, S, KV_W), BF16), tok(KV_W)),
        (jax.ShapeDtypeStruct((B, S, KV_W), BF16), tok(KV_W)),
        (jax.ShapeDtypeStruct((B, KV_W, S), BF16), tr(KV_W)),
        (jax.ShapeDtypeStruct((B, KV_W, S), BF16), tr(KV_W)),
        (jax.ShapeDtypeStruct((B, LANE, S), F32), tr(LANE)),
        (jax.ShapeDtypeStruct((B, MLA_HEADS * MLA_QK, S), BF16), tr(MLA_HEADS * MLA_QK)),
        (jax.ShapeDtypeStruct((B, S, MLA_HEADS * MLA_QK), BF16), tok(MLA_HEADS * MLA_QK)),
        (jax.ShapeDtypeStruct((B, MLA_HEADS * V_DIM, S), BF16), tr(MLA_HEADS * V_DIM)),
    ]
    return pl.pallas_call(
        _inproj_kernel,
        grid=(B, S // tm),
        in_specs=[tok(D), full(g1), full(win_p), full(qng), full(kvng), full(wqa), full(wqb),
                  full(wk), full(wv), full(prope), tok(LANE), tok(LANE), tok(LANE)],
        out_specs=[o[1] for o in outs],
        out_shape=[o[0] for o in outs],
        compiler_params=_cparams("parallel", "parallel"),
        name="inproj",
    )(x, g1, win_p, qng, kvng, wqa, wqb, wk, wv, prope, cq_tab, sq_tab, ck_tab)


def _gelu(x):
    return 0.5 * x * (1.0 + lax.erf(x * (1.0 / math.sqrt(2.0))))


def _compress_kernel(c_ref, plo_ref, phi_ref, w1lo_ref, w1hi_ref, w2_ref, out_ref):
    c = c_ref[0, 0]
    nc = c.shape[0]
    a = _dot((c + plo_ref[...]).astype(BF16), w1lo_ref[0])
    b = _dot((c + phi_ref[...]).astype(BF16), w1hi_ref[0])
    hid = a + pltpu.roll(b, nc - 1, 0)
    out_ref[0, 0] = _dot(_gelu(hid).astype(BF16), w2_ref[0]).astype(BF16)


def _compress(c_all, pos_lo, pos_hi, w1lo, w1hi, w2):
    _, BG, nc, W = c_all.shape
    return pl.pallas_call(
        _compress_kernel,
        grid=(2, BG),
        in_specs=[pl.BlockSpec((1, 1, nc, W), lambda t, i: (t, i, 0, 0)),
                  pl.BlockSpec(pos_lo.shape, lambda t, i: (0, 0)),
                  pl.BlockSpec(pos_hi.shape, lambda t, i: (0, 0)),
                  pl.BlockSpec((1,) + w1lo.shape[1:], lambda t, i: (t, 0, 0)),
                  pl.BlockSpec((1,) + w1hi.shape[1:], lambda t, i: (t, 0, 0)),
                  pl.BlockSpec((1,) + w2.shape[1:], lambda t, i: (t, 0, 0))],
        out_specs=pl.BlockSpec((1, 1, nc, HEAD_DIM), lambda t, i: (t, i, 0, 0)),
        out_shape=jax.ShapeDtypeStruct((2, BG, nc, HEAD_DIM), BF16),
        compiler_params=_cparams("parallel", "parallel"),
        name="compress",
    )(c_all, pos_lo, pos_hi, w1lo, w1hi, w2)


def _nsa_cmp_kernel(qT_ref, kc_ref, vcT_ref, ft_ref, ovT_ref, ocT_ref, selb_ref):
    qi = pl.program_id(2)
    tq = qT_ref.shape[2]
    nc = kc_ref.shape[1]
    nsel = ovT_ref.shape[0]
    k = kc_ref[0]
    vT = vcT_ref[0]
    start = pl.multiple_of(nc - (tq // CMP_STRIDE) * qi, 8)
    psum = jnp.zeros((nc, tq), F32)
    scores = [_dot(k, qT_ref[0, r * HEAD_DIM:(r + 1) * HEAD_DIM, :]) for r in range(NSA_REP)]
    for r in range(NSA_REP):
        s = scores[r] + jnp.concatenate(
            [ft_ref[r, pl.ds(pl.multiple_of(start - (LANE // CMP_STRIDE) * a, 8), nc), :]
             for a in range(tq // LANE)], axis=1)
        m = jnp.max(s, axis=0, keepdims=True)
        p = jnp.exp2(s - m)
        l = jnp.sum(p, axis=0, keepdims=True)
        p = p * jnp.where(m > 0.5 * NEG, 1.0 / l, 0.0)
        ocT_ref[0, r * HEAD_DIM:(r + 1) * HEAD_DIM, :] = _dot(vT, p.astype(BF16))
        psum = psum + p
    p_hi = psum.astype(BF16)
    p_lo = (psum - p_hi.astype(F32)).astype(BF16)
    imp = _dot(ovT_ref[...], p_hi) + _dot(ovT_ref[...], p_lo)

    sid = lax.broadcasted_iota(jnp.int32, (nsel, tq), 0)
    t = qi * tq + lax.broadcasted_iota(jnp.int32, (nsel, tq), 1)
    cur = jnp.right_shift(t, int(math.log2(SEL_LEN)))
    valid = sid <= cur
    forced = (sid == 0) | (sid == cur) | (sid == cur - 1)
    n_free = SEL_TOPN - 1 - (cur[0:1] >= 1).astype(jnp.int32) - (cur[0:1] >= 2).astype(jnp.int32)
    work = jnp.where(valid & jnp.logical_not(forced), imp, NEG)
    tau = jnp.zeros((1, tq), F32)
    for it in range(1, SEL_TOPN):
        mx = jnp.max(work, axis=0, keepdims=True)
        if it >= SEL_TOPN - 3:
            tau = jnp.where(n_free == it, mx, tau)
        if it < SEL_TOPN - 1:
            work = jnp.where(work >= mx, REMOVED, work)
    sel = valid & (forced | (imp >= tau))
    selb_ref[0, 0] = jnp.where(sel, 0.0, NEG)


def _nsa_cmp(qT, kcmp, vcmpT, ft, ovT):
    B, _, S = qT.shape
    G = NSA_KV_HEADS
    nc = kcmp.shape[1]
    nsel = ovT.shape[0]
    tq = TQ_CMP
    gw = NSA_REP * HEAD_DIM
    return pl.pallas_call(
        _nsa_cmp_kernel,
        grid=(B, G, S // tq),
        in_specs=[pl.BlockSpec((1, gw, tq), lambda b, g, i: (b, g, i)),
                  pl.BlockSpec((1, nc, HEAD_DIM), lambda b, g, i: (b * G + g, 0, 0)),
                  pl.BlockSpec((1, HEAD_DIM, nc), lambda b, g, i: (b * G + g, 0, 0)),
                  pl.BlockSpec((NSA_REP, 2 * nc, LANE), lambda b, g, i: (g, 0, 0)),
                  pl.BlockSpec(ovT.shape, lambda b, g, i: (0, 0))],
        out_specs=[pl.BlockSpec((1, gw, tq), lambda b, g, i: (b, g, i)),
                   pl.BlockSpec((1, 1, nsel, tq), lambda b, g, i: (b, g, 0, i))],
        out_shape=[jax.ShapeDtypeStruct((B, NSA_W, S), F32),
                   jax.ShapeDtypeStruct((B, G, nsel, S), F32)],
        compiler_params=_cparams("parallel", "parallel", "parallel"),
        name="nsa_cmp",
    )(qT, kcmp, vcmpT, ft, ovT)


def _sub_all(x8, op):
    for sh in (4, 2, 1):
        x8 = op(x8, pltpu.roll(x8, sh, 0))
    return x8


def _col_max8(s):
    return jnp.max(s.reshape(s.shape[0] // 8, 8, s.shape[1]), axis=0)


def _fa_init(m_ref, al_ref, acc_ref):
    m_ref[...] = jnp.full(m_ref.shape, REMOVED, F32)
    al_ref[...] = jnp.ones(al_ref.shape, F32)
    acc_ref[...] = jnp.zeros(acc_ref.shape, F32)


def _fa_update_m(st, idx, mx8):
    m_ref, al_ref = st[0], st[1]
    m_old = m_ref[idx]
    m_new = jnp.maximum(m_old, _sub_all(mx8, jnp.maximum))
    al_ref[idx] = jnp.exp2(m_old - m_new)
    m_ref[idx] = m_new


def _fa_pass2(s_buf, vT, st, idx):
    m_ref, al_ref, acc_ref = st
    tk, tq = s_buf.shape
    dv = acc_ref.shape[1]
    p3 = jnp.exp2(s_buf[...].reshape(tk // 8, 8, tq) - m_ref[idx][None])
    pv = _dot(vT, p3.reshape(tk, tq).astype(BF16))
    acc_ref[idx] = (acc_ref[idx].reshape(dv // 8, 8, tq) * al_ref[idx][None]).reshape(dv, tq) + pv


def _fa_result(st, idx, dv):
    acc = st[2][idx]
    tq = acc.shape[1]
    inv = 1.0 / acc[dv:dv + 8]
    return (acc[:dv].reshape(dv // 8, 8, tq) * inv[None]).reshape(dv, tq)


def _with_ones_rows(vT5):
    return jnp.concatenate([vT5, jnp.ones(vT5.shape[:-2] + (V_ONES, vT5.shape[-1]), vT5.dtype)], axis=-2)


def _fa_segment(c0, n, pass1, pass2, update):
    last = c0 + n - 1
    update(pass1(c0, 0))

    def body(i, carry):
        c = c0 + 2 * i
        mx = pass1(c + 1, 1)
        pass2(c, 0)
        update(mx)
        mx = pass1(jnp.where(c + 2 <= last, c + 2, c), 0)
        pass2(c + 1, 1)
        update(mx)
        return carry

    lax.fori_loop(0, n // 2, body, 0)


def _nsa_sel_kernel(qT_ref, ks_ref, vsT_ref, selb_ref, ts_ref, kw_ref, vwT_ref, tw_ref, osT_ref, owT_ref,
                    s_ref, m_ref, al_ref, acc_ref):
    qi = pl.program_id(2)
    tq = qT_ref.shape[2]
    st = (m_ref, al_ref, acc_ref)
    blocks = TK // SEL_LEN

    def table_bias(t_ref, r, base):
        return jnp.concatenate([t_ref[r, pl.ds(pl.multiple_of(base - LANE * a, LANE), TK), :]
                                for a in range(tq // LANE)], axis=1)

    def make_pass1(near):
        def pass1(c, slot):
            k = ks_ref[0, 0, pl.ds(pl.multiple_of(c * TK, TK), TK), :]
            mb = selb_ref[0, 0, pl.ds(pl.multiple_of((c // 2) * 2 * blocks, 2 * blocks), 2 * blocks), :]
            mb = mb[slot * blocks:(slot + 1) * blocks]
            out = []
            for r in range(NSA_REP):
                s = _dot(k, qT_ref[0, r * HEAD_DIM:(r + 1) * HEAD_DIM, :])
                if near:
                    s = s + table_bias(ts_ref, r, SEL_ROFF - TK * (2 * qi - c))
                s = jnp.concatenate(
                    [s[j * SEL_LEN:(j + 1) * SEL_LEN] + mb[j:j + 1] for j in range(blocks)], axis=0)
                s_ref[slot, r] = s
                out.append(_col_max8(s))
            return out
        return pass1

    def win_pass1(c, slot):
        k = kw_ref[0, 0, pl.ds(pl.multiple_of(c * TK, TK), TK), :]
        out = []
        for r in range(NSA_REP):
            s = _dot(k, qT_ref[0, r * HEAD_DIM:(r + 1) * HEAD_DIM, :])
            s = s + table_bias(tw_ref, r, WIN_ROFF2 - TK * (2 * qi - c))
            s_ref[slot, r] = s
            out.append(_col_max8(s))
        return out

    def make_pass2(vT_ref):
        def pass2(c, slot):
            vT = vT_ref[0, 0, c]
            for r in range(NSA_REP):
                _fa_pass2(s_ref.at[slot, r], vT, st, r)
        return pass2

    def update(mx):
        for r in range(NSA_REP):
            _fa_update_m(st, r, mx[r])

    _fa_init(*st)
    n_far = jnp.maximum(0, 2 * qi - SEL_WMAX)

    @pl.when(n_far > 0)
    def _():
        _fa_segment(0, n_far, make_pass1(False), make_pass2(vsT_ref), update)

    _fa_segment(n_far, 2 * qi + 2 - n_far, make_pass1(True), make_pass2(vsT_ref), update)
    for r in range(NSA_REP):
        osT_ref[0, r * HEAD_DIM:(r + 1) * HEAD_DIM, :] = _fa_result(st, r, HEAD_DIM)

    _fa_init(*st)
    w0 = jnp.maximum(0, 2 * qi - WIN_CHUNKS_BACK)
    _fa_segment(w0, 2 * qi + 2 - w0, win_pass1, make_pass2(vwT_ref), update)
    for r in range(NSA_REP):
        owT_ref[0, r * HEAD_DIM:(r + 1) * HEAD_DIM, :] = _fa_result(st, r, HEAD_DIM)


def _nsa_sel(qT, ks4, vsT5, selb, ts, kw4, vwT5, tw):
    B, _, S = qT.shape
    G = NSA_KV_HEADS
    nsel = selb.shape[2]
    gw = NSA_REP * HEAD_DIM
    kspec = pl.BlockSpec((1, 1, S, HEAD_DIM), lambda b, g, i: (b, g, 0, 0))
    vspec = pl.BlockSpec((1, 1, S // TK, HEAD_DIM + V_ONES, TK), lambda b, g, i: (b, g, 0, 0, 0))
    tspec = lambda t: pl.BlockSpec((NSA_REP,) + t.shape[1:], lambda b, g, i: (g, 0, 0),
                                   pipeline_mode=pl.Buffered(1))
    ospec = pl.BlockSpec((1, gw, TQ), lambda b, g, i: (b, g, i))
    return pl.pallas_call(
        _nsa_sel_kernel,
        grid=(B, G, S // TQ),
        in_specs=[pl.BlockSpec((1, gw, TQ), lambda b, g, i: (b, g, i)), kspec, vspec,
                  pl.BlockSpec((1, 1, nsel, TQ), lambda b, g, i: (b, g, 0, i)), tspec(ts),
                  kspec, vspec, tspec(tw)],
        out_specs=[ospec, ospec],
        out_shape=[jax.ShapeDtypeStruct((B, NSA_W, S), F32)] * 2,
        scratch_shapes=[pltpu.VMEM((2, NSA_REP, TK, TQ), F32)]
        + [pltpu.VMEM((NSA_REP, 8, TQ), F32)] * 2 + [pltpu.VMEM((NSA_REP, HEAD_DIM + V_ONES, TQ), F32)],
        compiler_params=_cparams("parallel", "parallel", "parallel"),
        name="nsa_sel_win",
    )(qT, ks4, vsT5, selb, ts, kw4, vwT5, tw)


MLA_HPS = 2


def _mla_kernel(qT_ref, k_ref, vT_ref, oT_ref, s_ref, m_ref, al_ref, acc_ref):
    qi = pl.program_id(2)
    tq = qT_ref.shape[2]
    st = (m_ref, al_ref, acc_ref)
    _fa_init(*st)

    def make_pass1(masked):
        def pass1(c, slot):
            out = []
            for h in range(MLA_HPS):
                k = k_ref[0, pl.ds(pl.multiple_of(c * TK, TK), TK), h * MLA_QK:(h + 1) * MLA_QK]
                s = _dot(k, qT_ref[0, h * MLA_QK:(h + 1) * MLA_QK, :])
                if masked:
                    kpos = c * TK + lax.broadcasted_iota(jnp.int32, (TK, tq), 0)
                    qpos = qi * TQ + lax.broadcasted_iota(jnp.int32, (TK, tq), 1)
                    s = jnp.where(kpos <= qpos, s, NEG)
                s_ref[slot, h] = s
                out.append(_col_max8(s))
            return out
        return pass1

    def pass2(c, slot):
        for h in range(MLA_HPS):
            _fa_pass2(s_ref.at[slot, h], vT_ref[0, h, c], st, h)

    def update(mx):
        for h in range(MLA_HPS):
            _fa_update_m(st, h, mx[h])

    @pl.when(qi > 0)
    def _():
        _fa_segment(0, 2 * qi, make_pass1(False), pass2, update)

    diag = make_pass1(True)
    update(diag(2 * qi, 0))
    mx = diag(2 * qi + 1, 1)
    pass2(2 * qi, 0)
    update(mx)
    pass2(2 * qi + 1, 1)
    for h in range(MLA_HPS):
        oT_ref[0, h * V_DIM:(h + 1) * V_DIM, :] = _fa_result(st, h, V_DIM)


def _mla(qmT, km, vmT5):
    B, _, S = qmT.shape
    hp = MLA_HPS
    return pl.pallas_call(
        _mla_kernel,
        grid=(B, MLA_HEADS // hp, S // TQ),
        in_specs=[pl.BlockSpec((1, hp * MLA_QK, TQ), lambda b, h, i: (b, h, i)),
                  pl.BlockSpec((1, S, hp * MLA_QK), lambda b, h, i: (b, 0, h)),
                  pl.BlockSpec((1, hp, S // TK, V_DIM + V_ONES, TK), lambda b, h, i: (b, h, 0, 0, 0))],
        out_specs=pl.BlockSpec((1, hp * V_DIM, TQ), lambda b, h, i: (b, h, i)),
        out_shape=jax.ShapeDtypeStruct((B, MLA_HEADS * V_DIM, S), F32),
        scratch_shapes=[pltpu.VMEM((2, hp, TK, TQ), F32)] + [pltpu.VMEM((hp, 8, TQ), F32)] * 2
        + [pltpu.VMEM((hp, V_DIM + V_ONES, TQ), F32)],
        compiler_params=_cparams("parallel", "parallel", "parallel"),
        name="mla",
    )(qmT, km, vmT5)


def _outproj_kernel(ocT_ref, osT_ref, owT_ref, omT_ref, gT_ref, x_ref, gn_ref, gm_ref, wout_ref,
                    g2_ref, x2_ref, hnT_ref):
    parts = []
    for h in range(NSA_HEADS):
        rows = slice(h * HEAD_DIM, (h + 1) * HEAD_DIM)
        parts.append(gT_ref[0, 3 * h:3 * h + 1, :] * ocT_ref[0, rows, :]
                     + gT_ref[0, 3 * h + 1:3 * h + 2, :] * osT_ref[0, rows, :]
                     + gT_ref[0, 3 * h + 2:3 * h + 3, :] * owT_ref[0, rows, :])
    nsaT = jnp.concatenate(parts, axis=0)

    def norm_t(yT, g):
        y = yT * lax.rsqrt(jnp.mean(yT * yT, axis=0, keepdims=True) + EPS)
        return (y.T * g).astype(BF16)

    y_nsa = norm_t(nsaT, gn_ref[...])
    y_mla = norm_t(omT_ref[0], gm_ref[...])
    x2 = x_ref[0] + _dot(y_nsa, wout_ref[:NSA_W, :]) + _dot(y_mla, wout_ref[NSA_W:, :])
    x2_ref[0] = x2
    hnT_ref[...] = (_rms(x2) * g2_ref[...]).T.astype(BF16)


def _outproj(ocT, osT, owT, omT, gT, x, gn, gm, wout, g2):
    B, S, D = x.shape
    tm = TM_OUT
    nt = S // tm
    tr = lambda a: pl.BlockSpec((1, a.shape[1], tm), lambda b, i: (b, 0, i))
    full = lambda a: pl.BlockSpec(a.shape, lambda b, i: (0,) * a.ndim)
    return pl.pallas_call(
        _outproj_kernel,
        grid=(B, nt),
        in_specs=[tr(ocT), tr(osT), tr(owT), tr(omT), tr(gT),
                  pl.BlockSpec((1, tm, D), lambda b, i: (b, i, 0)),
                  full(gn), full(gm), full(wout), full(g2)],
        out_specs=[pl.BlockSpec((1, tm, D), lambda b, i: (b, i, 0)),
                   pl.BlockSpec((D, tm), lambda b, i: (0, b * nt + i))],
        out_shape=[jax.ShapeDtypeStruct((B, S, D), F32),
                   jax.ShapeDtypeStruct((D, B * S), BF16)],
        compiler_params=_cparams("parallel", "parallel"),
        name="outproj",
    )(ocT, osT, owT, omT, gT, x, gn, gm, wout, g2)


def _row_max_bcast(w3):
    m8 = jnp.max(w3, axis=0)
    for sh in (4, 2, 1):
        m8 = jnp.maximum(m8, pltpu.roll(m8, sh, 0))
    return m8


def _top_values(s, n):
    w3 = s.reshape(s.shape[0] // 8, 8, s.shape[1])
    vals = []
    for it in range(n):
        mx = _row_max_bcast(w3)
        vals.append(mx)
        if it < n - 1:
            w3 = jnp.where(w3 >= mx[None], REMOVED, w3)
    return vals


def _pair_list(n):
    return [(i, j) for i in range(n) for j in range(n) if (i + 1) * (j + 1) <= n]


def _top_pair_sums(a, b, n):
    cands = [a[i] + b[j] for i, j in _pair_list(n)]
    vals = []
    for it in range(n):
        mx = functools.reduce(jnp.maximum, cands)
        vals.append(mx)
        if it < n - 1:
            cands = [jnp.where(c >= mx, REMOVED, c) for c in cands]
    return vals


def _peer_topk_kernel(hnT_ref, wqT_ref, k1_ref, k2_ref, c_ref, s2_ref, e1_ref, b1_ref):
    tm = hnT_ref.shape[1]
    n = PEER_TOPK + 1
    qT = _dot(wqT_ref[...], hnT_ref[...]).astype(BF16)
    dk = k1_ref.shape[1]
    sub = lax.broadcasted_iota(jnp.int32, (8, tm), 0)
    a_m = [jnp.zeros((8, tm), F32)] * n
    b_m = [jnp.zeros((8, tm), F32)] * n
    for h in range(PEER_HEADS):
        s1 = _dot(k1_ref[...], qT[2 * h * dk:(2 * h + 1) * dk])
        s2 = _dot(k2_ref[...], qT[(2 * h + 1) * dk:(2 * h + 2) * dk])
        c_ref[h] = s1
        s2_ref[h] = s2
        a = _top_values(s1, n)
        b = _top_values(s2, n)
        a_m = [jnp.where(sub == h, a[i], a_m[i]) for i in range(n)]
        b_m = [jnp.where(sub == h, b[i], b_m[i]) for i in range(n)]
    v = _top_pair_sums(a_m, b_m, n)
    z = functools.reduce(jnp.add, [jnp.exp(v[i] - v[0]) for i in range(PEER_TOPK)])
    thr = 0.5 * (v[PEER_TOPK - 1] + v[PEER_TOPK])
    for h in range(PEER_HEADS):
        c_ref[h] = thr[h:h + 1, :] - c_ref[h]
    e1_ref[...] = thr - a_m[0] - jnp.log(z)
    b1_ref[...] = b_m[0]


def _peer_topk(hnT, wqT, k1, k2):
    D, N = hnT.shape
    tm = TM_TOPK
    H = PEER_HEADS
    nk = k1.shape[0]
    full = lambda a: pl.BlockSpec(a.shape, lambda i: (0,) * a.ndim)
    return pl.pallas_call(
        _peer_topk_kernel,
        grid=(N // tm,),
        in_specs=[pl.BlockSpec((D, tm), lambda i: (0, i)), full(wqT), full(k1), full(k2)],
        out_specs=[pl.BlockSpec((H, nk, tm), lambda i: (0, 0, i)),
                   pl.BlockSpec((H, nk, tm), lambda i: (0, 0, i)),
                   pl.BlockSpec((H, tm), lambda i: (0, i)),
                   pl.BlockSpec((H, tm), lambda i: (0, i))],
        out_shape=[jax.ShapeDtypeStruct((H, nk, N), F32),
                   jax.ShapeDtypeStruct((H, nk, N), F32),
                   jax.ShapeDtypeStruct((H, N), F32),
                   jax.ShapeDtypeStruct((H, N), F32)],
        compiler_params=_cparams("parallel"),
        name="peer_topk",
    )(hnT, wqT, k1, k2)


def _peer_dense_kernel(hnT_ref, u_ref, vT_ref, c_ref, s2_ref, e1_ref, b1_ref, x2_ref, gf_ref, out_ref,
                       acc_ref, p1_ref, p2_ref, pc_ref):
    j = pl.program_id(1)
    nk = s2_ref.shape[1]
    ne = u_ref.shape[0]
    tm = hnT_ref.shape[1]

    @pl.when(j == 0)
    def _():
        acc_ref[...] = jnp.zeros(acc_ref.shape, F32)
        for h in range(PEER_HEADS):
            p1_ref[h] = jnp.exp(e1_ref[h:h + 1, :] - c_ref[h])
            p2_ref[h] = jnp.exp(s2_ref[h] - b1_ref[h:h + 1, :])
            pc_ref[h] = jnp.exp(c_ref[h] - b1_ref[h:h + 1, :])

    act = _dot(u_ref[...], hnT_ref[...])
    ws = []
    for kk in range(ne // nk):
        i1 = j * (ne // nk) + kk
        gate = jnp.zeros((nk, tm), F32)
        for h in range(PEER_HEADS):
            p2 = p2_ref[h]
            sel = p2 >= pc_ref[h, pl.ds(i1, 1), :]
            gate = gate + jnp.where(sel, p1_ref[h, pl.ds(i1, 1), :] * p2, 0.0)
        ws.append((gate * _gelu(act[kk * nk:(kk + 1) * nk])).astype(BF16))
    acc_ref[...] += _dot(vT_ref[...], jnp.concatenate(ws, axis=0))

    @pl.when(j == pl.num_programs(1) - 1)
    def _():
        y = x2_ref[...] + acc_ref[...].T
        out_ref[...] = _rms(y) * gf_ref[...]


def _peer_dense(hnT, u, vT, c, s2, e1, b1, x2, gf):
    D, N = hnT.shape
    NE = u.shape[0]
    tm, ne = TM_PEER, NE_PEER
    H, nk, _ = s2.shape
    return pl.pallas_call(
        _peer_dense_kernel,
        grid=(N // tm, NE // ne),
        in_specs=[pl.BlockSpec((D, tm), lambda i, j: (0, i)),
                  pl.BlockSpec((ne, D), lambda i, j: (j, 0)),
                  pl.BlockSpec((D, ne), lambda i, j: (0, j)),
                  pl.BlockSpec((H, nk, tm), lambda i, j: (0, 0, i)),
                  pl.BlockSpec((H, nk, tm), lambda i, j: (0, 0, i)),
                  pl.BlockSpec((H, tm), lambda i, j: (0, i)),
                  pl.BlockSpec((H, tm), lambda i, j: (0, i)),
                  pl.BlockSpec((tm, D), lambda i, j: (i, 0)),
                  pl.BlockSpec(gf.shape, lambda i, j: (0, 0))],
        out_specs=pl.BlockSpec((tm, D), lambda i, j: (i, 0)),
        out_shape=jax.ShapeDtypeStruct((N, D), F32),
        scratch_shapes=[pltpu.VMEM((D, tm), F32)] + [pltpu.VMEM((H, nk, tm), F32)] * 3,
        compiler_params=_cparams("parallel", "arbitrary"),
        name="peer_dense",
    )(hnT, u, vT, c, s2, e1, b1, x2, gf)


def _t5_bucket(dist):
    dist = jnp.maximum(dist, 0)
    max_exact = N_BUCKETS // 2
    d = jnp.maximum(dist, 1).astype(F32)
    large = max_exact + (jnp.log(d / max_exact) / math.log(MAX_DISTANCE / max_exact)
                         * (N_BUCKETS - max_exact)).astype(jnp.int32)
    return jnp.where(dist < max_exact, dist, jnp.minimum(large, N_BUCKETS - 1))


def _toeplitz(fn, rows, roff):
    nb = rows // LANE
    x = (np.arange(2 * LANE) + LANE - 1) % (2 * LANE)
    d = LANE * (nb - 1 - np.arange(nb))[:, None] + x[None, :] - (rows - 1) + roff
    seg = fn(jnp.asarray(d.reshape(-1), jnp.int32))
    H = seg.shape[0]
    seg = seg.reshape(H, nb, 2 * LANE)
    toep = jnp.tile(seg, (1, 1, LANE))[..., :LANE * (2 * LANE - 1)]
    toep = toep.reshape(H, nb, LANE, 2 * LANE - 1)[..., :LANE]
    return toep.reshape(H, rows, LANE)


def _bias_tables(t5_bias, nc):
    onehot = lambda d: (_t5_bucket(d)[:, None] == jnp.arange(N_BUCKETS)[None, :]).astype(F32)

    def rel(d):
        return jnp.einsum("nb,bh->hn", onehot(d), t5_bias.astype(F32), precision=lax.Precision.HIGHEST)

    far = t5_bias[N_BUCKETS - 1].astype(F32)[:, None]
    nk16 = LANE // CMP_STRIDE
    n16 = 2 * nc + nk16 - 1
    x = CMP_STRIDE * (n16 - 1 - np.arange(n16))[:, None] + np.arange(CMP_STRIDE)[None, :]
    d = jnp.asarray((x - CMP_STRIDE * (nc - 1) - (CMP_LEN - 1)).reshape(-1), jnp.int32)
    f16 = jnp.where(d >= 0, rel(d) * LOG2E, NEG).reshape(t5_bias.shape[1], n16, CMP_STRIDE)
    ft = jnp.concatenate([f16[:, nk16 - 1 - k:nk16 - 1 - k + 2 * nc] for k in range(nk16)], axis=-1)
    ts = _toeplitz(lambda d: jnp.where(d >= 0, (rel(d) - far) * LOG2E, NEG), SEL_ROWS, SEL_ROFF)
    tw = _toeplitz(lambda d: jnp.where((d >= 0) & (d < WINDOW), rel(d) * LOG2E, NEG), WIN_ROWS2, WIN_ROFF2)
    return ft, ts, tw


def _overlap_t(nc, n_cmp, nsel):
    cs = np.arange(nc) * CMP_STRIDE
    ce = cs + CMP_LEN - 1
    ss = np.arange(nsel) * SEL_LEN
    ov = (cs[None, :] < ss[:, None] + SEL_LEN) & (ce[None, :] >= ss[:, None]) & (np.arange(nc)[None, :] < n_cmp)
    return jnp.asarray(ov, BF16)


def kernel(x, positions, norm1_g, w_in, cmp_pos, cmp_k_w1, cmp_k_w2, cmp_v_w1, cmp_v_w2, t5_bias,
           q_norm_g, w_q_up, kv_norm_g, w_kv_up, grp_norm_nsa, grp_norm_mla, w_out, norm2_g,
           peer_wq, peer_keys1, peer_keys2, peer_u, peer_v, final_g):
    B, S, D = x.shape
    G, H = NSA_KV_HEADS, MLA_HEADS
    assert S % TQ == 0 and S % TM_IN == 0 and norm1_g.shape[0] == 1
    nc = S // CMP_STRIDE
    n_cmp = (S - CMP_LEN) // CMP_STRIDE + 1
    nsel = S // SEL_LEN
    half = QK_ROPE // 2

    w = w_in[0]
    o_g = NSA_W + 6 * KV_W
    n_gate = 3 * NSA_HEADS
    o_cq = o_g + n_gate
    q_lora = w_q_up.shape[1]
    kv_lora = w_kv_up.shape[1]
    o_ckv = o_cq + q_lora
    o_kr = o_ckv + kv_lora
    kr1, kr2 = w[:, o_kr:o_kr + half], w[:, o_kr + half:o_kr + 2 * half]
    misc = jnp.concatenate([kr1, kr2, -kr2, kr1, w[:, o_g:o_g + n_gate],
                            jnp.zeros((D, LANE - 4 * half - n_gate), F32)], axis=1)
    win_p = jnp.concatenate([w[:, :o_g], w[:, o_cq:o_kr], misc], axis=1).astype(BF16)

    wq = w_q_up[0].reshape(q_lora, H, QK_NOPE + QK_ROPE)
    q1, q2 = wq[..., QK_NOPE:QK_NOPE + half], wq[..., QK_NOPE + half:]
    zpad = jnp.zeros((q_lora, H, MLA_QK - QK_NOPE - QK_ROPE), F32)
    wqa = jnp.concatenate([wq[..., :QK_NOPE], q1, q2, zpad], -1).reshape(q_lora, H * MLA_QK).astype(BF16)
    wqb = jnp.concatenate([jnp.zeros_like(wq[..., :QK_NOPE]), -q2, q1, zpad], -1).reshape(
        q_lora, H * MLA_QK).astype(BF16)
    wkv = w_kv_up[0].reshape(kv_lora, H, QK_NOPE + V_DIM)
    wk = jnp.concatenate([wkv[..., :QK_NOPE], jnp.zeros((kv_lora, H, MLA_QK - QK_NOPE), F32)],
                         -1).reshape(kv_lora, H * MLA_QK).astype(BF16)
    wv = wkv[..., QK_NOPE:].reshape(kv_lora, H * V_DIM).astype(BF16)
    pr = np.zeros((LANE, H, MLA_QK), np.float32)
    for j in range(QK_ROPE):
        pr[j, :, QK_NOPE + j] = 1.0
        pr[QK_ROPE + j, :, QK_NOPE + j] = 1.0
    prope = jnp.asarray(pr.reshape(LANE, H * MLA_QK), BF16)

    inv = ROPE_THETA ** (-jnp.arange(half, dtype=F32) / half)
    ang = positions.astype(F32)[..., None] * inv
    cos, sin = jnp.cos(ang), jnp.sin(ang)
    one = jnp.ones((B, S, QK_NOPE), F32)
    zq = jnp.zeros((B, S, MLA_QK - QK_NOPE - QK_ROPE), F32)
    mla_scale = (QK_NOPE + QK_ROPE) ** -0.5 * LOG2E
    cq_tab = jnp.concatenate([one, cos, cos, zq], -1) * mla_scale
    sq_tab = jnp.concatenate([jnp.zeros_like(one), sin, sin, zq], -1) * mla_scale
    ck_tab = jnp.concatenate([cos, cos, sin, sin, jnp.zeros((B, S, LANE - 4 * half), F32)], -1)

    row = lambda v: v.reshape(1, -1).astype(F32)
    (qT, kc, vc, ks, kw, vsT, vwT, gmT, qmT, km, vmT) = _inproj(
        x, row(norm1_g[0]), win_p, row(q_norm_g[0]), row(kv_norm_g[0]), wqa, wqb, wk, wv, prope,
        cq_tab, sq_tab, ck_tab)

    def chunks16(a):
        return a.reshape(B, nc, CMP_STRIDE, G, HEAD_DIM).transpose(0, 3, 1, 2, 4).reshape(
            B * G, nc, CMP_STRIDE * HEAD_DIM)

    def heads_first(a):
        return a.reshape(B, S, G, HEAD_DIM).transpose(0, 2, 1, 3)

    def lane_chunks(aT, nh, dh, ck):
        L = aT.shape[-1]
        return aT.reshape(B, nh, dh, L // ck, ck).transpose(0, 1, 3, 2, 4)

    c_all = jnp.stack([chunks16(kc), chunks16(vc)])
    pos_lo = cmp_pos[0, :CMP_STRIDE].reshape(1, -1)
    pos_hi = cmp_pos[0, CMP_STRIDE:].reshape(1, -1)
    split = CMP_STRIDE * HEAD_DIM
    w1 = jnp.stack([cmp_k_w1[0], cmp_v_w1[0]]).astype(BF16)
    w2 = jnp.stack([cmp_k_w2[0], cmp_v_w2[0]]).astype(BF16)
    cmp_out = _compress(c_all, pos_lo, pos_hi, w1[:, :split], w1[:, split:], w2)
    kcmp = cmp_out[0]
    vcmpT = cmp_out[1].transpose(0, 2, 1)

    ft, ts, tw = _bias_tables(t5_bias, nc)
    ocT, selb = _nsa_cmp(qT, kcmp, vcmpT, ft, _overlap_t(nc, n_cmp, nsel))
    osT, owT = _nsa_sel(qT, heads_first(ks), _with_ones_rows(lane_chunks(vsT, G, HEAD_DIM, TK)), selb, ts,
                        heads_first(kw), _with_ones_rows(lane_chunks(vwT, G, HEAD_DIM, TK)), tw)
    omT = _mla(qmT, km, _with_ones_rows(lane_chunks(vmT, H, V_DIM, TK)))

    o_gate = 4 * half
    x2, hnT = _outproj(ocT, osT, owT, omT, gmT[:, o_gate:o_gate + n_gate], x, row(grp_norm_nsa[0]),
                       row(grp_norm_mla[0]), w_out[0].astype(BF16), row(norm2_g[0]))

    c, s2, e1, b1 = _peer_topk(hnT, peer_wq[0].T.astype(BF16), peer_keys1[0].astype(BF16),
                               peer_keys2[0].astype(BF16))
    out = _peer_dense(hnT, peer_u[0].astype(BF16), peer_v[0].T.astype(BF16), c, s2, e1, b1,
                      x2.reshape(B * S, D), row(final_g))
    return out.reshape(B, S, D)
```

```python
import functools
import math

import numpy as np
import jax
import jax.numpy as jnp
from jax import lax
from jax.experimental import pallas as pl
from jax.experimental.pallas import tpu as pltpu

F32 = jnp.float32
BF16 = jnp.bfloat16

HEAD_DIM = 64
NSA_HEADS = 8
NSA_KV_HEADS = 2
NSA_REP = NSA_HEADS // NSA_KV_HEADS
CMP_LEN = 32
CMP_STRIDE = 16
SEL_LEN = 64
SEL_TOPN = 16
WINDOW = 512
SEL_FORCE = 1e4
MLA_HEADS = 8
QK_NOPE = 64
QK_ROPE = 32
V_DIM = 64
ROPE_THETA = 10000.0
N_BUCKETS = 32
MAX_DISTANCE = 2048
PEER_HEADS = 8
PEER_TOPK = 16
EPS = 1e-6
NEG = -1e30
REMOVED = -3e38

LANE = 128
NSA_W = NSA_HEADS * HEAD_DIM
KV_W = NSA_KV_HEADS * HEAD_DIM
MLA_QK = 128
FAR_DIST = 2048

TM_IN = 512
TQ_CMP = 256
TQ = 512
TK = 256
TM_OUT = 256
TM_TOPK = 256
TM_PEER = 512
NE_PEER = 512
SEL_WMAX = (FAR_DIST + TK - 1) // TK
SEL_ROFF = TK * SEL_WMAX + TQ - LANE
SEL_ROWS = SEL_ROFF + 2 * TK
WIN_CHUNKS_BACK = WINDOW // TK
WIN_ROFF2 = TK * WIN_CHUNKS_BACK + TQ - LANE
WIN_ROWS2 = WIN_ROFF2 + 2 * TK
LOG2E = math.log2(math.e)
V_ONES = 16
VMEM_LIMIT = 56 * 1024 * 1024


def _cparams(*sem):
    return pltpu.CompilerParams(dimension_semantics=sem, vmem_limit_bytes=VMEM_LIMIT)


def _rms(x):
    return x * lax.rsqrt(jnp.mean(x * x, axis=-1, keepdims=True) + EPS)


def _dot(a, b):
    return jnp.dot(a, b, preferred_element_type=F32)


def _inproj_kernel(x_ref, g1_ref, win_ref, qng_ref, kvng_ref, wqa_ref, wqb_ref, wk_ref, wv_ref,
                   prope_ref, cq_ref, sq_ref, ck_ref,
                   qT_ref, kc_ref, vc_ref, ks_ref, kw_ref, vsT_ref, vwT_ref, gT_ref,
                   qmT_ref, km_ref, vmT_ref):
    x = x_ref[0]
    h = _rms(x) * g1_ref[...]
    z = _dot(h.astype(BF16), win_ref[...])
    qT_ref[0] = (z[:, :NSA_W] * (HEAD_DIM ** -0.5 * LOG2E)).T.astype(BF16)
    o = NSA_W
    kc_ref[0] = z[:, o:o + KV_W]
    vc_ref[0] = z[:, o + KV_W:o + 2 * KV_W]
    ks_ref[0] = z[:, o + 2 * KV_W:o + 3 * KV_W].astype(BF16)
    vsT_ref[0] = z[:, o + 3 * KV_W:o + 4 * KV_W].T.astype(BF16)
    kw_ref[0] = z[:, o + 4 * KV_W:o + 5 * KV_W].astype(BF16)
    vwT_ref[0] = z[:, o + 5 * KV_W:o + 6 * KV_W].T.astype(BF16)
    o = o + 6 * KV_W
    cq = z[:, o:o + 256]
    ckv = z[:, o + 256:o + 384]
    misc = z[:, o + 384:o + 512]
    gT_ref[0] = jax.nn.sigmoid(misc).T
    cqn = (_rms(cq) * qng_ref[...]).astype(BF16)
    cos_q = jnp.tile(cq_ref[0], (1, MLA_HEADS))
    sin_q = jnp.tile(sq_ref[0], (1, MLA_HEADS))
    qm = _dot(cqn, wqa_ref[...]) * cos_q + _dot(cqn, wqb_ref[...]) * sin_q
    qmT_ref[0] = qm.T.astype(BF16)
    ckvn = (_rms(ckv) * kvng_ref[...]).astype(BF16)
    kr = (misc * ck_ref[0]).astype(BF16)
    km_ref[0] = (_dot(ckvn, wk_ref[...]) + _dot(kr, prope_ref[...])).astype(BF16)
    vmT_ref[0] = _dot(ckvn, wv_ref[...]).T.astype(BF16)


def _inproj(x, g1, win_p, qng, kvng, wqa, wqb, wk, wv, prope, cq_tab, sq_tab, ck_tab):
    B, S, D = x.shape
    tm = TM_IN
    full = lambda a: pl.BlockSpec(a.shape, lambda b, i: (0,) * a.ndim)
    tok = lambda w: pl.BlockSpec((1, tm, w), lambda b, i: (b, i, 0))
    tr = lambda w: pl.BlockSpec((1, w, tm), lambda b, i: (b, 0, i))
    outs = [
        (jax.ShapeDtypeStruct((B, NSA_W, S), BF16), tr(NSA_W)),
        (jax.ShapeDtypeStruct((B, S, KV_W), F32), tok(KV_W)),
        (jax.ShapeDtypeStruct((B, S, KV_W), F32), tok(KV_W)),
        (jax.ShapeDtypeStruct((B, S, KV_W), BF16), tok(KV_W)),
        (jax.ShapeDtypeStruct((B, S, KV_W), BF16), tok(KV_W)),
        (jax.ShapeDtypeStruct((B, KV_W, S), BF16), tr(KV_W)),
        (jax.ShapeDtypeStruct((B, KV_W, S), BF16), tr(KV_W)),
        (jax.ShapeDtypeStruct((B, LANE, S), F32), tr(LANE)),
        (jax.ShapeDtypeStruct((B, MLA_HEADS * MLA_QK, S), BF16), tr(MLA_HEADS * MLA_QK)),
        (jax.ShapeDtypeStruct((B, S, MLA_HEADS * MLA_QK), BF16), tok(MLA_HEADS * MLA_QK)),
        (jax.ShapeDtypeStruct((B, MLA_HEADS * V_DIM, S), BF16), tr(MLA_HEADS * V_DIM)),
    ]
    return pl.pallas_call(
        _inproj_kernel,
        grid=(B, S // tm),
        in_specs=[tok(D), full(g1), full(win_p), full(qng), full(kvng), full(wqa), full(wqb),
                  full(wk), full(wv), full(prope), tok(LANE), tok(LANE), tok(LANE)],
        out_specs=[o[1] for o in outs],
        out_shape=[o[0] for o in outs],
        compiler_params=_cparams("parallel", "parallel"),
        name="inproj",
    )(x, g1, win_p, qng, kvng, wqa, wqb, wk, wv, prope, cq_tab, sq_tab, ck_tab)


def _gelu(x):
    return 0.5 * x * (1.0 + lax.erf(x * (1.0 / math.sqrt(2.0))))


def _compress_kernel(c_ref, plo_ref, phi_ref, w1lo_ref, w1hi_ref, w2_ref, out_ref):
    c = c_ref[0, 0]
    nc = c.shape[0]
    a = _dot((c + plo_ref[...]).astype(BF16), w1lo_ref[0])
    b = _dot((c + phi_ref[...]).astype(BF16), w1hi_ref[0])
    hid = a + pltpu.roll(b, nc - 1, 0)
    out_ref[0, 0] = _dot(_gelu(hid).astype(BF16), w2_ref[0]).astype(BF16)


def _compress(c_all, pos_lo, pos_hi, w1lo, w1hi, w2):
    _, BG, nc, W = c_all.shape
    return pl.pallas_call(
        _compress_kernel,
        grid=(2, BG),
        in_specs=[pl.BlockSpec((1, 1, nc, W), lambda t, i: (t, i, 0, 0)),
                  pl.BlockSpec(pos_lo.shape, lambda t, i: (0, 0)),
                  pl.BlockSpec(pos_hi.shape, lambda t, i: (0, 0)),
                  pl.BlockSpec((1,) + w1lo.shape[1:], lambda t, i: (t, 0, 0)),
                  pl.BlockSpec((1,) + w1hi.shape[1:], lambda t, i: (t, 0, 0)),
                  pl.BlockSpec((1,) + w2.shape[1:], lambda t, i: (t, 0, 0))],
        out_specs=pl.BlockSpec((1, 1, nc, HEAD_DIM), lambda t, i: (t, i, 0, 0)),
        out_shape=jax.ShapeDtypeStruct((2, BG, nc, HEAD_DIM), BF16),
        compiler_params=_cparams("parallel", "parallel"),
        name="compress",
    )(c_all, pos_lo, pos_hi, w1lo, w1hi, w2)


def _nsa_cmp_kernel(qT_ref, kc_ref, vcT_ref, ft_ref, ovT_ref, ocT_ref, selb_ref):
    qi = pl.program_id(2)
    tq = qT_ref.shape[2]
    nc = kc_ref.shape[1]
    nsel = ovT_ref.shape[0]
    k = kc_ref[0]
    vT = vcT_ref[0]
    start = pl.multiple_of(nc - (tq // CMP_STRIDE) * qi, 8)
    psum = jnp.zeros((nc, tq), F32)
    scores = [_dot(k, qT_ref[0, r * HEAD_DIM:(r + 1) * HEAD_DIM, :]) for r in range(NSA_REP)]
    for r in range(NSA_REP):
        s = scores[r] + jnp.concatenate(
            [ft_ref[r, pl.ds(pl.multiple_of(start - (LANE // CMP_STRIDE) * a, 8), nc), :]
             for a in range(tq // LANE)], axis=1)
        m = jnp.max(s, axis=0, keepdims=True)
        p = jnp.exp2(s - m)
        l = jnp.sum(p, axis=0, keepdims=True)
        p = p * jnp.where(m > 0.5 * NEG, 1.0 / l, 0.0)
        ocT_ref[0, r * HEAD_DIM:(r + 1) * HEAD_DIM, :] = _dot(vT, p.astype(BF16))
        psum = psum + p
    p_hi = psum.astype(BF16)
    p_lo = (psum - p_hi.astype(F32)).astype(BF16)
    imp = _dot(ovT_ref[...], p_hi) + _dot(ovT_ref[...], p_lo)

    sid = lax.broadcasted_iota(jnp.int32, (nsel, tq), 0)
    t = qi * tq + lax.broadcasted_iota(jnp.int32, (nsel, tq), 1)
    cur = jnp.right_shift(t, int(math.log2(SEL_LEN)))
    valid = sid <= cur
    forced = (sid == 0) | (sid == cur) | (sid == cur - 1)
    n_free = SEL_TOPN - 1 - (cur[0:1] >= 1).astype(jnp.int32) - (cur[0:1] >= 2).astype(jnp.int32)
    work = jnp.where(valid & jnp.logical_not(forced), imp, NEG)
    tau = jnp.zeros((1, tq), F32)
    for it in range(1, SEL_TOPN):
        mx = jnp.max(work, axis=0, keepdims=True)
        if it >= SEL_TOPN - 3:
            tau = jnp.where(n_free == it, mx, tau)
        if it < SEL_TOPN - 1:
            work = jnp.where(work >= mx, REMOVED, work)
    sel = valid & (forced | (imp >= tau))
    selb_ref[0, 0] = jnp.where(sel, 0.0, NEG)


def _nsa_cmp(qT, kcmp, vcmpT, ft, ovT):
    B, _, S = qT.shape
    G = NSA_KV_HEADS
    nc = kcmp.shape[1]
    nsel = ovT.shape[0]
    tq = TQ_CMP
    gw = NSA_REP * HEAD_DIM
    return pl.pallas_call(
        _nsa_cmp_kernel,
        grid=(B, G, S // tq),
        in_specs=[pl.BlockSpec((1, gw, tq), lambda b, g, i: (b, g, i)),
                  pl.BlockSpec((1, nc, HEAD_DIM), lambda b, g, i: (b * G + g, 0, 0)),
                  pl.BlockSpec((1, HEAD_DIM, nc), lambda b, g, i: (b * G + g, 0, 0)),
                  pl.BlockSpec((NSA_REP, 2 * nc, LANE), lambda b, g, i: (g, 0, 0)),
                  pl.BlockSpec(ovT.shape, lambda b, g, i: (0, 0))],
        out_specs=[pl.BlockSpec((1, gw, tq), lambda b, g, i: (b, g, i)),
                   pl.BlockSpec((1, 1, nsel, tq), lambda b, g, i: (b, g, 0, i))],
        out_shape=[jax.ShapeDtypeStruct((B, NSA_W, S), F32),
                   jax.ShapeDtypeStruct((B, G, nsel, S), F32)],
        compiler_params=_cparams("parallel", "parallel", "parallel"),
        name="nsa_cmp",
    )(qT, kcmp, vcmpT, ft, ovT)


def _sub_all(x8, op):
    for sh in (4, 2, 1):
        x8 = op(x8, pltpu.roll(x8, sh, 0))
    return x8


def _col_max8(s):
    return jnp.max(s.reshape(s.shape[0] // 8, 8, s.shape[1]), axis=0)


def _fa_init(m_ref, al_ref, acc_ref):
    m_ref[...] = jnp.full(m_ref.shape, REMOVED, F32)
    al_ref[...] = jnp.ones(al_ref.shape, F32)
    acc_ref[...] = jnp.zeros(acc_ref.shape, F32)


def _fa_update_m(st, idx, mx8):
    m_ref, al_ref = st[0], st[1]
    m_old = m_ref[idx]
    m_new = jnp.maximum(m_old, _sub_all(mx8, jnp.maximum))
    al_ref[idx] = jnp.exp2(m_old - m_new)
    m_ref[idx] = m_new


def _fa_pass2(s_buf, vT, st, idx):
    m_ref, al_ref, acc_ref = st
    tk, tq = s_buf.shape
    dv = acc_ref.shape[1]
    p3 = jnp.exp2(s_buf[...].reshape(tk // 8, 8, tq) - m_ref[idx][None])
    pv = _dot(vT, p3.reshape(tk, tq).astype(BF16))
    acc_ref[idx] = (acc_ref[idx].reshape(dv // 8, 8, tq) * al_ref[idx][None]).reshape(dv, tq) + pv


def _fa_result(st, idx, dv):
    acc = st[2][idx]
    tq = acc.shape[1]
    inv = 1.0 / acc[dv:dv + 8]
    return (acc[:dv].reshape(dv // 8, 8, tq) * inv[None]).reshape(dv, tq)


def _with_ones_rows(vT5):
    return jnp.concatenate([vT5, jnp.ones(vT5.shape[:-2] + (V_ONES, vT5.shape[-1]), vT5.dtype)], axis=-2)


def _fa_segment(c0, n, pass1, pass2, update):
    last = c0 + n - 1
    update(pass1(c0, 0))

    def body(i, carry):
        c = c0 + 2 * i
        mx = pass1(c + 1, 1)
        pass2(c, 0)
        update(mx)
        mx = pass1(c + 2, 0)
        pass2(c + 1, 1)
        update(mx)
        return carry

    lax.fori_loop(0, n // 2 - 1, body, 0)
    mx = pass1(last, 1)
    pass2(last - 1, 0)
    update(mx)
    pass2(last, 1)


def _nsa_sel_kernel(qT_ref, ks_ref, vsT_ref, selb_ref, ts_ref, kw_ref, vwT_ref, tw_ref, osT_ref, owT_ref,
                    s_ref, m_ref, al_ref, acc_ref):
    qi = pl.program_id(2)
    tq = qT_ref.shape[2]
    st = (m_ref, al_ref, acc_ref)
    blocks = TK // SEL_LEN

    def table_bias(t_ref, r, base):
        return jnp.concatenate([t_ref[r, pl.ds(pl.multiple_of(base - LANE * a, LANE), TK), :]
                                for a in range(tq // LANE)], axis=1)

    def make_pass1(near):
        def pass1(c, slot):
            k = ks_ref[0, 0, pl.ds(pl.multiple_of(c * TK, TK), TK), :]
            mb = selb_ref[0, 0, pl.ds(pl.multiple_of((c // 2) * 2 * blocks, 2 * blocks), 2 * blocks), :]
            mb = mb[slot * blocks:(slot + 1) * blocks]
            out = []
            for r in range(NSA_REP):
                s = _dot(k, qT_ref[0, r * HEAD_DIM:(r + 1) * HEAD_DIM, :])
                if near:
                    s = s + table_bias(ts_ref, r, SEL_ROFF - TK * (2 * qi - c))
                s = jnp.concatenate(
                    [s[j * SEL_LEN:(j + 1) * SEL_LEN] + mb[j:j + 1] for j in range(blocks)], axis=0)
                s_ref[slot, r] = s
                out.append(_col_max8(s))
            return out
        return pass1

    def win_pass1(c, slot):
        k = kw_ref[0, 0, pl.ds(pl.multiple_of(c * TK, TK), TK), :]
        out = []
        for r in range(NSA_REP):
            s = _dot(k, qT_ref[0, r * HEAD_DIM:(r + 1) * HEAD_DIM, :])
            s = s + table_bias(tw_ref, r, WIN_ROFF2 - TK * (2 * qi - c))
            s_ref[slot, r] = s
            out.append(_col_max8(s))
        return out

    def make_pass2(vT_ref):
        def pass2(c, slot):
            vT = vT_ref[0, 0, c]
            for r in range(NSA_REP):
                _fa_pass2(s_ref.at[slot, r], vT, st, r)
        return pass2

    def update(mx):
        for r in range(NSA_REP):
            _fa_update_m(st, r, mx[r])

    _fa_init(*st)
    n_far = jnp.maximum(0, 2 * qi - SEL_WMAX)

    @pl.when(n_far > 0)
    def _():
        _fa_segment(0, n_far, make_pass1(False), make_pass2(vsT_ref), update)

    _fa_segment(n_far, 2 * qi + 2 - n_far, make_pass1(True), make_pass2(vsT_ref), update)
    for r in range(NSA_REP):
        osT_ref[0, r * HEAD_DIM:(r + 1) * HEAD_DIM, :] = _fa_result(st, r, HEAD_DIM)

    _fa_init(*st)
    w0 = jnp.maximum(0, 2 * qi - WIN_CHUNKS_BACK)
    _fa_segment(w0, 2 * qi + 2 - w0, win_pass1, make_pass2(vwT_ref), update)
    for r in range(NSA_REP):
        owT_ref[0, r * HEAD_DIM:(r + 1) * HEAD_DIM, :] = _fa_result(st, r, HEAD_DIM)


def _nsa_sel(qT, ks4, vsT5, selb, ts, kw4, vwT5, tw):
    B, _, S = qT.shape
    G = NSA_KV_HEADS
    nsel = selb.shape[2]
    gw = NSA_REP * HEAD_DIM
    kspec = pl.BlockSpec((1, 1, S, HEAD_DIM), lambda b, g, i: (b, g, 0, 0))
    vspec = pl.BlockSpec((1, 1, S // TK, HEAD_DIM + V_ONES, TK), lambda b, g, i: (b, g, 0, 0, 0))
    tspec = lambda t: pl.BlockSpec((NSA_REP,) + t.shape[1:], lambda b, g, i: (g, 0, 0),
                                   pipeline_mode=pl.Buffered(1))
    ospec = pl.BlockSpec((1, gw, TQ), lambda b, g, i: (b, g, i))
    return pl.pallas_call(
        _nsa_sel_kernel,
        grid=(B, G, S // TQ),
        in_specs=[pl.BlockSpec((1, gw, TQ), lambda b, g, i: (b, g, i)), kspec, vspec,
                  pl.BlockSpec((1, 1, nsel, TQ), lambda b, g, i: (b, g, 0, i)), tspec(ts),
                  kspec, vspec, tspec(tw)],
        out_specs=[ospec, ospec],
        out_shape=[jax.ShapeDtypeStruct((B, NSA_W, S), F32)] * 2,
        scratch_shapes=[pltpu.VMEM((2, NSA_REP, TK, TQ), F32)]
        + [pltpu.VMEM((NSA_REP, 8, TQ), F32)] * 2 + [pltpu.VMEM((NSA_REP, HEAD_DIM + V_ONES, TQ), F32)],
        compiler_params=_cparams("parallel", "parallel", "parallel"),
        name="nsa_sel_win",
    )(qT, ks4, vsT5, selb, ts, kw4, vwT5, tw)


MLA_HPS = 2


def _mla_kernel(qT_ref, k_ref, vT_ref, oT_ref, s_ref, m_ref, al_ref, acc_ref):
    qi = pl.program_id(2)
    tq = qT_ref.shape[2]
    st = (m_ref, al_ref, acc_ref)
    _fa_init(*st)

    def make_pass1(masked):
        def pass1(c, slot):
            out = []
            for h in range(MLA_HPS):
                k = k_ref[0, pl.ds(pl.multiple_of(c * TK, TK), TK), h * MLA_QK:(h + 1) * MLA_QK]
                s = _dot(k, qT_ref[0, h * MLA_QK:(h + 1) * MLA_QK, :])
                if masked:
                    kpos = c * TK + lax.broadcasted_iota(jnp.int32, (TK, tq), 0)
                    qpos = qi * TQ + lax.broadcasted_iota(jnp.int32, (TK, tq), 1)
                    s = jnp.where(kpos <= qpos, s, NEG)
                s_ref[slot, h] = s
                out.append(_col_max8(s))
            return out
        return pass1

    def pass2(c, slot):
        for h in range(MLA_HPS):
            _fa_pass2(s_ref.at[slot, h], vT_ref[0, h, c], st, h)

    def update(mx):
        for h in range(MLA_HPS):
            _fa_update_m(st, h, mx[h])

    @pl.when(qi > 0)
    def _():
        _fa_segment(0, 2 * qi, make_pass1(False), pass2, update)

    _fa_segment(2 * qi, 2, make_pass1(True), pass2, update)
    for h in range(MLA_HPS):
        oT_ref[0, h * V_DIM:(h + 1) * V_DIM, :] = _fa_result(st, h, V_DIM)


def _mla(qmT, km, vmT5):
    B, _, S = qmT.shape
    hp = MLA_HPS
    return pl.pallas_call(
        _mla_kernel,
        grid=(B, MLA_HEADS // hp, S // TQ),
        in_specs=[pl.BlockSpec((1, hp * MLA_QK, TQ), lambda b, h, i: (b, h, i)),
                  pl.BlockSpec((1, S, hp * MLA_QK), lambda b, h, i: (b, 0, h)),
                  pl.BlockSpec((1, hp, S // TK, V_DIM + V_ONES, TK), lambda b, h, i: (b, h, 0, 0, 0))],
        out_specs=pl.BlockSpec((1, hp * V_DIM, TQ), lambda b, h, i: (b, h, i)),
        out_shape=jax.ShapeDtypeStruct((B, MLA_HEADS * V_DIM, S), F32),
        scratch_shapes=[pltpu.VMEM((2, hp, TK, TQ), F32)] + [pltpu.VMEM((hp, 8, TQ), F32)] * 2
        + [pltpu.VMEM((hp, V_DIM + V_ONES, TQ), F32)],
        compiler_params=_cparams("parallel", "parallel", "parallel"),
        name="mla",
    )(qmT, km, vmT5)


def _outproj_kernel(ocT_ref, osT_ref, owT_ref, omT_ref, gT_ref, x_ref, gn_ref, gm_ref, wout_ref,
                    g2_ref, x2_ref, hnT_ref):
    parts = []
    for h in range(NSA_HEADS):
        rows = slice(h * HEAD_DIM, (h + 1) * HEAD_DIM)
        parts.append(gT_ref[0, 3 * h:3 * h + 1, :] * ocT_ref[0, rows, :]
                     + gT_ref[0, 3 * h + 1:3 * h + 2, :] * osT_ref[0, rows, :]
                     + gT_ref[0, 3 * h + 2:3 * h + 3, :] * owT_ref[0, rows, :])
    nsaT = jnp.concatenate(parts, axis=0)

    def norm_t(yT, g):
        y = yT * lax.rsqrt(jnp.mean(yT * yT, axis=0, keepdims=True) + EPS)
        return (y.T * g).astype(BF16)

    y_nsa = norm_t(nsaT, gn_ref[...])
    y_mla = norm_t(omT_ref[0], gm_ref[...])
    x2 = x_ref[0] + _dot(y_nsa, wout_ref[:NSA_W, :]) + _dot(y_mla, wout_ref[NSA_W:, :])
    x2_ref[0] = x2
    hnT_ref[...] = (_rms(x2) * g2_ref[...]).T.astype(BF16)


def _outproj(ocT, osT, owT, omT, gT, x, gn, gm, wout, g2):
    B, S, D = x.shape
    tm = TM_OUT
    nt = S // tm
    tr = lambda a: pl.BlockSpec((1, a.shape[1], tm), lambda b, i: (b, 0, i))
    full = lambda a: pl.BlockSpec(a.shape, lambda b, i: (0,) * a.ndim)
    return pl.pallas_call(
        _outproj_kernel,
        grid=(B, nt),
        in_specs=[tr(ocT), tr(osT), tr(owT), tr(omT), tr(gT),
                  pl.BlockSpec((1, tm, D), lambda b, i: (b, i, 0)),
                  full(gn), full(gm), full(wout), full(g2)],
        out_specs=[pl.BlockSpec((1, tm, D), lambda b, i: (b, i, 0)),
                   pl.BlockSpec((D, tm), lambda b, i: (0, b * nt + i))],
        out_shape=[jax.ShapeDtypeStruct((B, S, D), F32),
                   jax.ShapeDtypeStruct((D, B * S), BF16)],
        compiler_params=_cparams("parallel", "parallel"),
        name="outproj",
    )(ocT, osT, owT, omT, gT, x, gn, gm, wout, g2)


def _row_max_bcast(w3):
    m8 = jnp.max(w3, axis=0)
    for sh in (4, 2, 1):
        m8 = jnp.maximum(m8, pltpu.roll(m8, sh, 0))
    return m8


def _top_values(s, n):
    w3 = s.reshape(s.shape[0] // 8, 8, s.shape[1])
    vals = []
    for it in range(n):
        mx = _row_max_bcast(w3)
        vals.append(mx)
        if it < n - 1:
            w3 = jnp.where(w3 >= mx[None], REMOVED, w3)
    return vals


def _pair_list(n):
    return [(i, j) for i in range(n) for j in range(n) if (i + 1) * (j + 1) <= n]


def _top_pair_sums(a, b, n):
    cands = [a[i] + b[j] for i, j in _pair_list(n)]
    vals = []
    for it in range(n):
        mx = functools.reduce(jnp.maximum, cands)
        vals.append(mx)
        if it < n - 1:
            cands = [jnp.where(c >= mx, REMOVED, c) for c in cands]
    return vals


def _peer_topk_kernel(hnT_ref, wqT_ref, k1_ref, k2_ref, c_ref, s2_ref, e1_ref, b1_ref):
    tm = hnT_ref.shape[1]
    n = PEER_TOPK + 1
    qT = _dot(wqT_ref[...], hnT_ref[...]).astype(BF16)
    dk = k1_ref.shape[1]
    sub = lax.broadcasted_iota(jnp.int32, (8, tm), 0)
    a_m = [jnp.zeros((8, tm), F32)] * n
    b_m = [jnp.zeros((8, tm), F32)] * n
    for h in range(PEER_HEADS):
        s1 = _dot(k1_ref[...], qT[2 * h * dk:(2 * h + 1) * dk])
        s2 = _dot(k2_ref[...], qT[(2 * h + 1) * dk:(2 * h + 2) * dk])
        c_ref[h] = s1
        s2_ref[h] = s2
        a = _top_values(s1, n)
        b = _top_values(s2, n)
        a_m = [jnp.where(sub == h, a[i], a_m[i]) for i in range(n)]
        b_m = [jnp.where(sub == h, b[i], b_m[i]) for i in range(n)]
    v = _top_pair_sums(a_m, b_m, n)
    z = functools.reduce(jnp.add, [jnp.exp(v[i] - v[0]) for i in range(PEER_TOPK)])
    thr = 0.5 * (v[PEER_TOPK - 1] + v[PEER_TOPK])
    for h in range(PEER_HEADS):
        c_ref[h] = thr[h:h + 1, :] - c_ref[h]
    e1_ref[...] = thr - a_m[0] - jnp.log(z)
    b1_ref[...] = b_m[0]


def _peer_topk(hnT, wqT, k1, k2):
    D, N = hnT.shape
    tm = TM_TOPK
    H = PEER_HEADS
    nk = k1.shape[0]
    full = lambda a: pl.BlockSpec(a.shape, lambda i: (0,) * a.ndim)
    return pl.pallas_call(
        _peer_topk_kernel,
        grid=(N // tm,),
        in_specs=[pl.BlockSpec((D, tm), lambda i: (0, i)), full(wqT), full(k1), full(k2)],
        out_specs=[pl.BlockSpec((H, nk, tm), lambda i: (0, 0, i)),
                   pl.BlockSpec((H, nk, tm), lambda i: (0, 0, i)),
                   pl.BlockSpec((H, tm), lambda i: (0, i)),
                   pl.BlockSpec((H, tm), lambda i: (0, i))],
        out_shape=[jax.ShapeDtypeStruct((H, nk, N), F32),
                   jax.ShapeDtypeStruct((H, nk, N), F32),
                   jax.ShapeDtypeStruct((H, N), F32),
                   jax.ShapeDtypeStruct((H, N), F32)],
        compiler_params=_cparams("parallel"),
        name="peer_topk",
    )(hnT, wqT, k1, k2)


def _peer_dense_kernel(hnT_ref, u_ref, vT_ref, c_ref, s2_ref, e1_ref, b1_ref, x2_ref, gf_ref, out_ref,
                       acc_ref, p1_ref, p2_ref, pc_ref):
    j = pl.program_id(1)
    nk = s2_ref.shape[1]
    ne = u_ref.shape[0]
    tm = hnT_ref.shape[1]

    @pl.when(j == 0)
    def _():
        acc_ref[...] = jnp.zeros(acc_ref.shape, F32)
        for h in range(PEER_HEADS):
            p1_ref[h] = jnp.exp(e1_ref[h:h + 1, :] - c_ref[h])
            p2_ref[h] = jnp.exp(s2_ref[h] - b1_ref[h:h + 1, :])
            pc_ref[h] = jnp.exp(c_ref[h] - b1_ref[h:h + 1, :])

    act = _dot(u_ref[...], hnT_ref[...])
    ws = []
    for kk in range(ne // nk):
        i1 = j * (ne // nk) + kk
        gate = jnp.zeros((nk, tm), F32)
        for h in range(PEER_HEADS):
            p2 = p2_ref[h]
            sel = p2 >= pc_ref[h, pl.ds(i1, 1), :]
            gate = gate + jnp.where(sel, p1_ref[h, pl.ds(i1, 1), :] * p2, 0.0)
        ws.append((gate * _gelu(act[kk * nk:(kk + 1) * nk])).astype(BF16))
    acc_ref[...] += _dot(vT_ref[...], jnp.concatenate(ws, axis=0))

    @pl.when(j == pl.num_programs(1) - 1)
    def _():
        y = x2_ref[...] + acc_ref[...].T
        out_ref[...] = _rms(y) * gf_ref[...]


def _peer_dense(hnT, u, vT, c, s2, e1, b1, x2, gf):
    D, N = hnT.shape
    NE = u.shape[0]
    tm, ne = TM_PEER, NE_PEER
    H, nk, _ = s2.shape
    return pl.pallas_call(
        _peer_dense_kernel,
        grid=(N // tm, NE // ne),
        in_specs=[pl.BlockSpec((D, tm), lambda i, j: (0, i)),
                  pl.BlockSpec((ne, D), lambda i, j: (j, 0)),
                  pl.BlockSpec((D, ne), lambda i, j: (0, j)),
                  pl.BlockSpec((H, nk, tm), lambda i, j: (0, 0, i)),
                  pl.BlockSpec((H, nk, tm), lambda i, j: (0, 0, i)),
                  pl.BlockSpec((H, tm), lambda i, j: (0, i)),
                  pl.BlockSpec((H, tm), lambda i, j: (0, i)),
                  pl.BlockSpec((tm, D), lambda i, j: (i, 0)),
                  pl.BlockSpec(gf.shape, lambda i, j: (0, 0))],
        out_specs=pl.BlockSpec((tm, D), lambda i, j: (i, 0)),
        out_shape=jax.ShapeDtypeStruct((N, D), F32),
        scratch_shapes=[pltpu.VMEM((D, tm), F32)] + [pltpu.VMEM((H, nk, tm), F32)] * 3,
        compiler_params=_cparams("parallel", "arbitrary"),
        name="peer_dense",
    )(hnT, u, vT, c, s2, e1, b1, x2, gf)


def _t5_bucket(dist):
    dist = jnp.maximum(dist, 0)
    max_exact = N_BUCKETS // 2
    d = jnp.maximum(dist, 1).astype(F32)
    large = max_exact + (jnp.log(d / max_exact) / math.log(MAX_DISTANCE / max_exact)
                         * (N_BUCKETS - max_exact)).astype(jnp.int32)
    return jnp.where(dist < max_exact, dist, jnp.minimum(large, N_BUCKETS - 1))


def _toeplitz(fn, rows, roff):
    nb = rows // LANE
    x = (np.arange(2 * LANE) + LANE - 1) % (2 * LANE)
    d = LANE * (nb - 1 - np.arange(nb))[:, None] + x[None, :] - (rows - 1) + roff
    seg = fn(jnp.asarray(d.reshape(-1), jnp.int32))
    H = seg.shape[0]
    seg = seg.reshape(H, nb, 2 * LANE)
    toep = jnp.tile(seg, (1, 1, LANE))[..., :LANE * (2 * LANE - 1)]
    toep = toep.reshape(H, nb, LANE, 2 * LANE - 1)[..., :LANE]
    return toep.reshape(H, rows, LANE)


def _bias_tables(t5_bias, nc):
    onehot = lambda d: (_t5_bucket(d)[:, None] == jnp.arange(N_BUCKETS)[None, :]).astype(F32)

    def rel(d):
        return jnp.einsum("nb,bh->hn", onehot(d), t5_bias.astype(F32), precision=lax.Precision.HIGHEST)

    far = t5_bias[N_BUCKETS - 1].astype(F32)[:, None]
    nk16 = LANE // CMP_STRIDE
    n16 = 2 * nc + nk16 - 1
    x = CMP_STRIDE * (n16 - 1 - np.arange(n16))[:, None] + np.arange(CMP_STRIDE)[None, :]
    d = jnp.asarray((x - CMP_STRIDE * (nc - 1) - (CMP_LEN - 1)).reshape(-1), jnp.int32)
    f16 = jnp.where(d >= 0, rel(d) * LOG2E, NEG).reshape(t5_bias.shape[1], n16, CMP_STRIDE)
    ft = jnp.concatenate([f16[:, nk16 - 1 - k:nk16 - 1 - k + 2 * nc] for k in range(nk16)], axis=-1)
    ts = _toeplitz(lambda d: jnp.where(d >= 0, (rel(d) - far) * LOG2E, NEG), SEL_ROWS, SEL_ROFF)
    tw = _toeplitz(lambda d: jnp.where((d >= 0) & (d < WINDOW), rel(d) * LOG2E, NEG), WIN_ROWS2, WIN_ROFF2)
    return ft, ts, tw


def _overlap_t(nc, n_cmp, nsel):
    cs = np.arange(nc) * CMP_STRIDE
    ce = cs + CMP_LEN - 1
    ss = np.arange(nsel) * SEL_LEN
    ov = (cs[None, :] < ss[:, None] + SEL_LEN) & (ce[None, :] >= ss[:, None]) & (np.arange(nc)[None, :] < n_cmp)
    return jnp.asarray(ov, BF16)


def kernel(x, positions, norm1_g, w_in, cmp_pos, cmp_k_w1, cmp_k_w2, cmp_v_w1, cmp_v_w2, t5_bias,
           q_norm_g, w_q_up, kv_norm_g, w_kv_up, grp_norm_nsa, grp_norm_mla, w_out, norm2_g,
           peer_wq, peer_keys1, peer_keys2, peer_u, peer_v, final_g):
    B, S, D = x.shape
    G, H = NSA_KV_HEADS, MLA_HEADS
    assert S % TQ == 0 and S % TM_IN == 0 and norm1_g.shape[0] == 1
    nc = S // CMP_STRIDE
    n_cmp = (S - CMP_LEN) // CMP_STRIDE + 1
    nsel = S // SEL_LEN
    half = QK_ROPE // 2

    w = w_in[0]
    o_g = NSA_W + 6 * KV_W
    n_gate = 3 * NSA_HEADS
    o_cq = o_g + n_gate
    q_lora = w_q_up.shape[1]
    kv_lora = w_kv_up.shape[1]
    o_ckv = o_cq + q_lora
    o_kr = o_ckv + kv_lora
    kr1, kr2 = w[:, o_kr:o_kr + half], w[:, o_kr + half:o_kr + 2 * half]
    misc = jnp.concatenate([kr1, kr2, -kr2, kr1, w[:, o_g:o_g + n_gate],
                            jnp.zeros((D, LANE - 4 * half - n_gate), F32)], axis=1)
    win_p = jnp.concatenate([w[:, :o_g], w[:, o_cq:o_kr], misc], axis=1).astype(BF16)

    wq = w_q_up[0].reshape(q_lora, H, QK_NOPE + QK_ROPE)
    q1, q2 = wq[..., QK_NOPE:QK_NOPE + half], wq[..., QK_NOPE + half:]
    zpad = jnp.zeros((q_lora, H, MLA_QK - QK_NOPE - QK_ROPE), F32)
    wqa = jnp.concatenate([wq[..., :QK_NOPE], q1, q2, zpad], -1).reshape(q_lora, H * MLA_QK).astype(BF16)
    wqb = jnp.concatenate([jnp.zeros_like(wq[..., :QK_NOPE]), -q2, q1, zpad], -1).reshape(
        q_lora, H * MLA_QK).astype(BF16)
    wkv = w_kv_up[0].reshape(kv_lora, H, QK_NOPE + V_DIM)
    wk = jnp.concatenate([wkv[..., :QK_NOPE], jnp.zeros((kv_lora, H, MLA_QK - QK_NOPE), F32)],
                         -1).reshape(kv_lora, H * MLA_QK).astype(BF16)
    wv = wkv[..., QK_NOPE:].reshape(kv_lora, H * V_DIM).astype(BF16)
    pr = np.zeros((LANE, H, MLA_QK), np.float32)
    for j in range(QK_ROPE):
        pr[j, :, QK_NOPE + j] = 1.0
        pr[QK_ROPE + j, :, QK_NOPE + j] = 1.0
    prope = jnp.asarray(pr.reshape(LANE, H * MLA_QK), BF16)

    inv = ROPE_THETA ** (-jnp.arange(half, dtype=F32) / half)
    pos = positions.astype(F32)[..., None]
    lane = np.arange(LANE)
    ang_q = pos * jnp.concatenate([jnp.zeros((QK_NOPE,), F32), inv, inv,
                                   jnp.zeros((MLA_QK - QK_NOPE - QK_ROPE,), F32)])
    mla_scale = (QK_NOPE + QK_ROPE) ** -0.5 * LOG2E
    cq_tab = jnp.cos(ang_q) * jnp.asarray(np.where(lane < QK_NOPE + QK_ROPE, mla_scale, 0.0), F32)
    sq_tab = jnp.sin(ang_q) * mla_scale
    ang_k = pos * jnp.concatenate([inv, inv, inv, inv, jnp.zeros((LANE - 4 * half,), F32)])
    ck_tab = jnp.where(lane < 2 * half, jnp.cos(ang_k), jnp.where(lane < 4 * half, jnp.sin(ang_k), 0.0))

    row = lambda v: v.reshape(1, -1).astype(F32)
    (qT, kc, vc, ks, kw, vsT, vwT, gmT, qmT, km, vmT) = _inproj(
        x, row(norm1_g[0]), win_p, row(q_norm_g[0]), row(kv_norm_g[0]), wqa, wqb, wk, wv, prope,
        cq_tab, sq_tab, ck_tab)

    def chunks16(a):
        return a.reshape(B, nc, CMP_STRIDE, G, HEAD_DIM).transpose(0, 3, 1, 2, 4).reshape(
            B * G, nc, CMP_STRIDE * HEAD_DIM)

    def heads_first(a):
        return a.reshape(B, S, G, HEAD_DIM).transpose(0, 2, 1, 3)

    def lane_chunks(aT, nh, dh, ck):
        L = aT.shape[-1]
        return aT.reshape(B, nh, dh, L // ck, ck).transpose(0, 1, 3, 2, 4)

    c_all = jnp.stack([chunks16(kc), chunks16(vc)])
    pos_lo = cmp_pos[0, :CMP_STRIDE].reshape(1, -1)
    pos_hi = cmp_pos[0, CMP_STRIDE:].reshape(1, -1)
    split = CMP_STRIDE * HEAD_DIM
    w1 = jnp.stack([cmp_k_w1[0], cmp_v_w1[0]]).astype(BF16)
    w2 = jnp.stack([cmp_k_w2[0], cmp_v_w2[0]]).astype(BF16)
    cmp_out = _compress(c_all, pos_lo, pos_hi, w1[:, :split], w1[:, split:], w2)
    kcmp = cmp_out[0]
    vcmpT = cmp_out[1].transpose(0, 2, 1)

    ft, ts, tw = _bias_tables(t5_bias, nc)
    ocT, selb = _nsa_cmp(qT, kcmp, vcmpT, ft, _overlap_t(nc, n_cmp, nsel))
    osT, owT = _nsa_sel(qT, heads_first(ks), _with_ones_rows(lane_chunks(vsT, G, HEAD_DIM, TK)), selb, ts,
                        heads_first(kw), _with_ones_rows(lane_chunks(vwT, G, HEAD_DIM, TK)), tw)
    omT = _mla(qmT, km, _with_ones_rows(lane_chunks(vmT, H, V_DIM, TK)))

    o_gate = 4 * half
    x2, hnT = _outproj(ocT, osT, owT, omT, gmT[:, o_gate:o_gate + n_gate], x, row(grp_norm_nsa[0]),
                       row(grp_norm_mla[0]), w_out[0].astype(BF16), row(norm2_g[0]))

    c, s2, e1, b1 = _peer_topk(hnT, peer_wq[0].T.astype(BF16), peer_keys1[0].astype(BF16),
                               peer_keys2[0].astype(BF16))
    out = _peer_dense(hnT, peer_u[0].astype(BF16), peer_v[0].T.astype(BF16), c, s2, e1, b1,
                      x2.reshape(B * S, D), row(final_g))
    return out.reshape(B, S, D)
```

```python
import functools
import math

import numpy as np
import jax
import jax.numpy as jnp
from jax import lax
from jax.experimental import pallas as pl
from jax.experimental.pallas import tpu as pltpu

F32 = jnp.float32
BF16 = jnp.bfloat16

HEAD_DIM = 64
NSA_HEADS = 8
NSA_KV_HEADS = 2
NSA_REP = NSA_HEADS // NSA_KV_HEADS
CMP_LEN = 32
CMP_STRIDE = 16
SEL_LEN = 64
SEL_TOPN = 16
WINDOW = 512
SEL_FORCE = 1e4
MLA_HEADS = 8
QK_NOPE = 64
QK_ROPE = 32
V_DIM = 64
ROPE_THETA = 10000.0
N_BUCKETS = 32
MAX_DISTANCE = 2048
PEER_HEADS = 8
PEER_TOPK = 16
EPS = 1e-6
NEG = -1e30
REMOVED = -3e38

LANE = 128
NSA_W = NSA_HEADS * HEAD_DIM
KV_W = NSA_KV_HEADS * HEAD_DIM
MLA_QK = 128
FAR_DIST = 2048

TM_IN = 512
TQ_CMP = 512
TQ = 512
TK = 256
TM_OUT = 256
TM_TOPK = 256
TM_PEER = 512
NE_PEER = 512
SEL_WMAX = (FAR_DIST + TK - 1) // TK
SEL_ROFF = TK * SEL_WMAX + TQ - LANE
SEL_ROWS = SEL_ROFF + 2 * TK
WIN_CHUNKS_BACK = WINDOW // TK
WIN_ROFF2 = TK * WIN_CHUNKS_BACK + TQ - LANE
WIN_ROWS2 = WIN_ROFF2 + 2 * TK
LOG2E = math.log2(math.e)
V_ONES = 16
VMEM_LIMIT = 56 * 1024 * 1024


def _cparams(*sem):
    return pltpu.CompilerParams(dimension_semantics=sem, vmem_limit_bytes=VMEM_LIMIT)


def _rms(x):
    return x * lax.rsqrt(jnp.mean(x * x, axis=-1, keepdims=True) + EPS)


def _dot(a, b):
    return jnp.dot(a, b, preferred_element_type=F32)


def _inproj_kernel(x_ref, g1_ref, win_ref, qng_ref, kvng_ref, wqa_ref, wqb_ref, wk_ref, wv_ref,
                   prope_ref, cq_ref, sq_ref, ck_ref,
                   qT_ref, kc_ref, vc_ref, ks_ref, kw_ref, vsT_ref, vwT_ref, gT_ref,
                   qmT_ref, km_ref, vmT_ref):
    x = x_ref[0]
    h = _rms(x) * g1_ref[...]
    z = _dot(h.astype(BF16), win_ref[...])
    qT_ref[0] = (z[:, :NSA_W] * (HEAD_DIM ** -0.5 * LOG2E)).T.astype(BF16)
    o = NSA_W
    kc_ref[0] = z[:, o:o + KV_W]
    vc_ref[0] = z[:, o + KV_W:o + 2 * KV_W]
    ks_ref[0] = z[:, o + 2 * KV_W:o + 3 * KV_W].astype(BF16)
    vsT_ref[0] = z[:, o + 3 * KV_W:o + 4 * KV_W].T.astype(BF16)
    kw_ref[0] = z[:, o + 4 * KV_W:o + 5 * KV_W].astype(BF16)
    vwT_ref[0] = z[:, o + 5 * KV_W:o + 6 * KV_W].T.astype(BF16)
    o = o + 6 * KV_W
    cq = z[:, o:o + 256]
    ckv = z[:, o + 256:o + 384]
    misc = z[:, o + 384:o + 512]
    gT_ref[0] = jax.nn.sigmoid(misc).T
    cqn = (_rms(cq) * qng_ref[...]).astype(BF16)
    cos_q = jnp.tile(cq_ref[0], (1, MLA_HEADS))
    sin_q = jnp.tile(sq_ref[0], (1, MLA_HEADS))
    qm = _dot(cqn, wqa_ref[...]) * cos_q + _dot(cqn, wqb_ref[...]) * sin_q
    qmT_ref[0] = qm.T.astype(BF16)
    ckvn = (_rms(ckv) * kvng_ref[...]).astype(BF16)
    kr = (misc * ck_ref[0]).astype(BF16)
    km_ref[0] = (_dot(ckvn, wk_ref[...]) + _dot(kr, prope_ref[...])).astype(BF16)
    vmT_ref[0] = _dot(ckvn, wv_ref[...]).T.astype(BF16)


def _inproj(x, g1, win_p, qng, kvng, wqa, wqb, wk, wv, prope, cq_tab, sq_tab, ck_tab):
    B, S, D = x.shape
    tm = TM_IN
    full = lambda a: pl.BlockSpec(a.shape, lambda b, i: (0,) * a.ndim)
    tok = lambda w: pl.BlockSpec((1, tm, w), lambda b, i: (b, i, 0))
    tr = lambda w: pl.BlockSpec((1, w, tm), lambda b, i: (b, 0, i))
    outs = [
        (jax.ShapeDtypeStruct((B, NSA_W, S), BF16), tr(NSA_W)),
        (jax.ShapeDtypeStruct((B, S, KV_W), F32), tok(KV_W)),
        (jax.ShapeDtypeStruct((B, S, KV_W), F32), tok(KV_W)),
        (jax.ShapeDtypeStruct((B, S, KV_W), BF16), tok(KV_W)),
        (jax.ShapeDtypeStruct((B, S, KV_W), BF16), tok(KV_W)),
        (jax.ShapeDtypeStruct((B, KV_W, S), BF16), tr(KV_W)),
        (jax.ShapeDtypeStruct((B, KV_W, S), BF16), tr(KV_W)),
        (jax.ShapeDtypeStruct((B, LANE, S), F32), tr(LANE)),
        (jax.ShapeDtypeStruct((B, MLA_HEADS * MLA_QK, S), BF16), tr(MLA_HEADS * MLA_QK)),
        (jax.ShapeDtypeStruct((B, S, MLA_HEADS * MLA_QK), BF16), tok(MLA_HEADS * MLA_QK)),
        (jax.ShapeDtypeStruct((B, MLA_HEADS * V_DIM, S), BF16), tr(MLA_HEADS * V_DIM)),
    ]
    return pl.pallas_call(
        _inproj_kernel,
        grid=(B, S // tm),
        in_specs=[tok(D), full(g1), full(win_p), full(qng), full(kvng), full(wqa), full(wqb),
                  full(wk), full(wv), full(prope), tok(LANE), tok(LANE), tok(LANE)],
        out_specs=[o[1] for o in outs],
        out_shape=[o[0] for o in outs],
        compiler_params=_cparams("parallel", "parallel"),
        name="inproj",
    )(x, g1, win_p, qng, kvng, wqa, wqb, wk, wv, prope, cq_tab, sq_tab, ck_tab)


def _gelu(x):
    return 0.5 * x * (1.0 + lax.erf(x * (1.0 / math.sqrt(2.0))))


def _compress_kernel(c_ref, plo_ref, phi_ref, w1lo_ref, w1hi_ref, w2_ref, out_ref):
    c = c_ref[0, 0]
    nc = c.shape[0]
    a = _dot((c + plo_ref[...]).astype(BF16), w1lo_ref[0])
    b = _dot((c + phi_ref[...]).astype(BF16), w1hi_ref[0])
    hid = a + pltpu.roll(b, nc - 1, 0)
    out_ref[0, 0] = _dot(_gelu(hid).astype(BF16), w2_ref[0]).astype(BF16)


def _compress(c_all, pos_lo, pos_hi, w1lo, w1hi, w2):
    _, BG, nc, W = c_all.shape
    return pl.pallas_call(
        _compress_kernel,
        grid=(2, BG),
        in_specs=[pl.BlockSpec((1, 1, nc, W), lambda t, i: (t, i, 0, 0)),
                  pl.BlockSpec(pos_lo.shape, lambda t, i: (0, 0)),
                  pl.BlockSpec(pos_hi.shape, lambda t, i: (0, 0)),
                  pl.BlockSpec((1,) + w1lo.shape[1:], lambda t, i: (t, 0, 0)),
                  pl.BlockSpec((1,) + w1hi.shape[1:], lambda t, i: (t, 0, 0)),
                  pl.BlockSpec((1,) + w2.shape[1:], lambda t, i: (t, 0, 0))],
        out_specs=pl.BlockSpec((1, 1, nc, HEAD_DIM), lambda t, i: (t, i, 0, 0)),
        out_shape=jax.ShapeDtypeStruct((2, BG, nc, HEAD_DIM), BF16),
        compiler_params=_cparams("parallel", "parallel"),
        name="compress",
    )(c_all, pos_lo, pos_hi, w1lo, w1hi, w2)


def _nsa_cmp_kernel(qT_ref, kc_ref, vcT_ref, ft_ref, ovT_ref, ocT_ref, selb_ref):
    qi = pl.program_id(2)
    tq = qT_ref.shape[2]
    nc = kc_ref.shape[1]
    nsel = ovT_ref.shape[0]
    k = kc_ref[0]
    vT = vcT_ref[0]
    start = pl.multiple_of(nc - (tq // CMP_STRIDE) * qi, 8)
    psum = jnp.zeros((nc, tq), F32)
    scores = [_dot(k, qT_ref[0, r * HEAD_DIM:(r + 1) * HEAD_DIM, :]) for r in range(NSA_REP)]
    for r in range(NSA_REP):
        s = scores[r] + jnp.concatenate(
            [ft_ref[r, pl.ds(pl.multiple_of(start - (LANE // CMP_STRIDE) * a, 8), nc), :]
             for a in range(tq // LANE)], axis=1)
        m = jnp.max(s, axis=0, keepdims=True)
        p = jnp.exp2(s - m)
        l = jnp.sum(p, axis=0, keepdims=True)
        p = p * jnp.where(m > 0.5 * NEG, 1.0 / l, 0.0)
        ocT_ref[0, r * HEAD_DIM:(r + 1) * HEAD_DIM, :] = _dot(vT, p.astype(BF16))
        psum = psum + p
    p_hi = psum.astype(BF16)
    p_lo = (psum - p_hi.astype(F32)).astype(BF16)
    imp = _dot(ovT_ref[...], p_hi) + _dot(ovT_ref[...], p_lo)

    sid = lax.broadcasted_iota(jnp.int32, (nsel, tq), 0)
    t = qi * tq + lax.broadcasted_iota(jnp.int32, (nsel, tq), 1)
    cur = jnp.right_shift(t, int(math.log2(SEL_LEN)))
    valid = sid <= cur
    forced = (sid == 0) | (sid == cur) | (sid == cur - 1)
    n_free = SEL_TOPN - 1 - (cur[0:1] >= 1).astype(jnp.int32) - (cur[0:1] >= 2).astype(jnp.int32)
    work = jnp.where(valid & jnp.logical_not(forced), imp, NEG)
    tau = jnp.zeros((1, tq), F32)
    for it in range(1, SEL_TOPN):
        mx = jnp.max(work, axis=0, keepdims=True)
        if it >= SEL_TOPN - 3:
            tau = jnp.where(n_free == it, mx, tau)
        if it < SEL_TOPN - 1:
            work = jnp.where(work >= mx, REMOVED, work)
    sel = valid & (forced | (imp >= tau))
    selb_ref[0, 0] = jnp.where(sel, 0.0, NEG)


def _nsa_cmp(qT, kcmp, vcmpT, ft, ovT):
    B, _, S = qT.shape
    G = NSA_KV_HEADS
    nc = kcmp.shape[1]
    nsel = ovT.shape[0]
    tq = TQ_CMP
    gw = NSA_REP * HEAD_DIM
    return pl.pallas_call(
        _nsa_cmp_kernel,
        grid=(B, G, S // tq),
        in_specs=[pl.BlockSpec((1, gw, tq), lambda b, g, i: (b, g, i)),
                  pl.BlockSpec((1, nc, HEAD_DIM), lambda b, g, i: (b * G + g, 0, 0)),
                  pl.BlockSpec((1, HEAD_DIM, nc), lambda b, g, i: (b * G + g, 0, 0)),
                  pl.BlockSpec((NSA_REP, 2 * nc, LANE), lambda b, g, i: (g, 0, 0)),
                  pl.BlockSpec(ovT.shape, lambda b, g, i: (0, 0))],
        out_specs=[pl.BlockSpec((1, gw, tq), lambda b, g, i: (b, g, i)),
                   pl.BlockSpec((1, 1, nsel, tq), lambda b, g, i: (b, g, 0, i))],
        out_shape=[jax.ShapeDtypeStruct((B, NSA_W, S), F32),
                   jax.ShapeDtypeStruct((B, G, nsel, S), F32)],
        compiler_params=_cparams("parallel", "parallel", "parallel"),
        name="nsa_cmp",
    )(qT, kcmp, vcmpT, ft, ovT)


def _sub_all(x8, op):
    for sh in (4, 2, 1):
        x8 = op(x8, pltpu.roll(x8, sh, 0))
    return x8


def _col_max8(s):
    return jnp.max(s.reshape(s.shape[0] // 8, 8, s.shape[1]), axis=0)


def _fa_init(m_ref, al_ref, acc_ref):
    m_ref[...] = jnp.full(m_ref.shape, REMOVED, F32)
    al_ref[...] = jnp.ones(al_ref.shape, F32)
    acc_ref[...] = jnp.zeros(acc_ref.shape, F32)


def _fa_update_m(st, idx, mx8):
    m_ref, al_ref = st[0], st[1]
    m_old = m_ref[idx]
    m_new = jnp.maximum(m_old, _sub_all(mx8, jnp.maximum))
    al_ref[idx] = jnp.exp2(m_old - m_new)
    m_ref[idx] = m_new


def _fa_pass2(s_buf, vT, st, idx):
    m_ref, al_ref, acc_ref = st
    tk, tq = s_buf.shape
    dv = acc_ref.shape[1]
    p3 = jnp.exp2(s_buf[...].reshape(tk // 8, 8, tq) - m_ref[idx][None])
    pv = _dot(vT, p3.reshape(tk, tq).astype(BF16))
    acc_ref[idx] = (acc_ref[idx].reshape(dv // 8, 8, tq) * al_ref[idx][None]).reshape(dv, tq) + pv


def _fa_result(st, idx, dv):
    acc = st[2][idx]
    tq = acc.shape[1]
    inv = 1.0 / acc[dv:dv + 8]
    return (acc[:dv].reshape(dv // 8, 8, tq) * inv[None]).reshape(dv, tq)


def _with_ones_rows(vT5):
    return jnp.concatenate([vT5, jnp.ones(vT5.shape[:-2] + (V_ONES, vT5.shape[-1]), vT5.dtype)], axis=-2)


def _fa_segment(c0, n, pass1, pass2, update):
    last = c0 + n - 1
    update(pass1(c0, 0))

    def body(i, carry):
        c = c0 + 2 * i
        mx = pass1(c + 1, 1)
        pass2(c, 0)
        update(mx)
        mx = pass1(c + 2, 0)
        pass2(c + 1, 1)
        update(mx)
        return carry

    lax.fori_loop(0, n // 2 - 1, body, 0)
    mx = pass1(last, 1)
    pass2(last - 1, 0)
    update(mx)
    pass2(last, 1)


def _nsa_sel_kernel(qT_ref, ks_ref, vsT_ref, selb_ref, ts_ref, kw_ref, vwT_ref, tw_ref, osT_ref, owT_ref,
                    s_ref, m_ref, al_ref, acc_ref):
    qi = pl.program_id(2)
    tq = qT_ref.shape[2]
    st = (m_ref, al_ref, acc_ref)
    blocks = TK // SEL_LEN

    def table_bias(t_ref, r, base):
        return jnp.concatenate([t_ref[r, pl.ds(pl.multiple_of(base - LANE * a, LANE), TK), :]
                                for a in range(tq // LANE)], axis=1)

    def make_pass1(near):
        def pass1(c, slot):
            k = ks_ref[0, 0, pl.ds(pl.multiple_of(c * TK, TK), TK), :]
            mb = selb_ref[0, 0, pl.ds(pl.multiple_of((c // 2) * 2 * blocks, 2 * blocks), 2 * blocks), :]
            mb = mb[slot * blocks:(slot + 1) * blocks]
            out = []
            for r in range(NSA_REP):
                s = _dot(k, qT_ref[0, r * HEAD_DIM:(r + 1) * HEAD_DIM, :])
                if near:
                    s = s + table_bias(ts_ref, r, SEL_ROFF - TK * (2 * qi - c))
                s = jnp.concatenate(
                    [s[j * SEL_LEN:(j + 1) * SEL_LEN] + mb[j:j + 1] for j in range(blocks)], axis=0)
                s_ref[slot, r] = s
                out.append(_col_max8(s))
            return out
        return pass1

    def win_pass1(c, slot):
        k = kw_ref[0, 0, pl.ds(pl.multiple_of(c * TK, TK), TK), :]
        out = []
        for r in range(NSA_REP):
            s = _dot(k, qT_ref[0, r * HEAD_DIM:(r + 1) * HEAD_DIM, :])
            s = s + table_bias(tw_ref, r, WIN_ROFF2 - TK * (2 * qi - c))
            s_ref[slot, r] = s
            out.append(_col_max8(s))
        return out

    def make_pass2(vT_ref):
        def pass2(c, slot):
            vT = vT_ref[0, 0, c]
            for r in range(NSA_REP):
                _fa_pass2(s_ref.at[slot, r], vT, st, r)
        return pass2

    def update(mx):
        for r in range(NSA_REP):
            _fa_update_m(st, r, mx[r])

    _fa_init(*st)
    n_far = jnp.maximum(0, 2 * qi - SEL_WMAX)

    @pl.when(n_far > 0)
    def _():
        _fa_segment(0, n_far, make_pass1(False), make_pass2(vsT_ref), update)

    _fa_segment(n_far, 2 * qi + 2 - n_far, make_pass1(True), make_pass2(vsT_ref), update)
    for r in range(NSA_REP):
        osT_ref[0, r * HEAD_DIM:(r + 1) * HEAD_DIM, :] = _fa_result(st, r, HEAD_DIM)

    _fa_init(*st)
    w0 = jnp.maximum(0, 2 * qi - WIN_CHUNKS_BACK)
    _fa_segment(w0, 2 * qi + 2 - w0, win_pass1, make_pass2(vwT_ref), update)
    for r in range(NSA_REP):
        owT_ref[0, r * HEAD_DIM:(r + 1) * HEAD_DIM, :] = _fa_result(st, r, HEAD_DIM)


def _nsa_sel(qT, ks4, vsT5, selb, ts, kw4, vwT5, tw):
    B, _, S = qT.shape
    G = NSA_KV_HEADS
    nsel = selb.shape[2]
    gw = NSA_REP * HEAD_DIM
    kspec = pl.BlockSpec((1, 1, S, HEAD_DIM), lambda b, g, i: (b, g, 0, 0))
    vspec = pl.BlockSpec((1, 1, S // TK, HEAD_DIM + V_ONES, TK), lambda b, g, i: (b, g, 0, 0, 0))
    tspec = lambda t: pl.BlockSpec((NSA_REP,) + t.shape[1:], lambda b, g, i: (g, 0, 0),
                                   pipeline_mode=pl.Buffered(1))
    ospec = pl.BlockSpec((1, gw, TQ), lambda b, g, i: (b, g, i))
    return pl.pallas_call(
        _nsa_sel_kernel,
        grid=(B, G, S // TQ),
        in_specs=[pl.BlockSpec((1, gw, TQ), lambda b, g, i: (b, g, i)), kspec, vspec,
                  pl.BlockSpec((1, 1, nsel, TQ), lambda b, g, i: (b, g, 0, i)), tspec(ts),
                  kspec, vspec, tspec(tw)],
        out_specs=[ospec, ospec],
        out_shape=[jax.ShapeDtypeStruct((B, NSA_W, S), F32)] * 2,
        scratch_shapes=[pltpu.VMEM((2, NSA_REP, TK, TQ), F32)]
        + [pltpu.VMEM((NSA_REP, 8, TQ), F32)] * 2 + [pltpu.VMEM((NSA_REP, HEAD_DIM + V_ONES, TQ), F32)],
        compiler_params=_cparams("parallel", "parallel", "parallel"),
        name="nsa_sel_win",
    )(qT, ks4, vsT5, selb, ts, kw4, vwT5, tw)


MLA_HPS = 2


def _mla_kernel(qT_ref, k_ref, vT_ref, oT_ref, s_ref, m_ref, al_ref, acc_ref):
    qi = pl.program_id(2)
    tq = qT_ref.shape[2]
    st = (m_ref, al_ref, acc_ref)
    _fa_init(*st)

    def make_pass1(masked):
        def pass1(c, slot):
            out = []
            for h in range(MLA_HPS):
                k = k_ref[0, pl.ds(pl.multiple_of(c * TK, TK), TK), h * MLA_QK:(h + 1) * MLA_QK]
                s = _dot(k, qT_ref[0, h * MLA_QK:(h + 1) * MLA_QK, :])
                if masked:
                    kpos = c * TK + lax.broadcasted_iota(jnp.int32, (TK, tq), 0)
                    qpos = qi * TQ + lax.broadcasted_iota(jnp.int32, (TK, tq), 1)
                    s = jnp.where(kpos <= qpos, s, NEG)
                s_ref[slot, h] = s
                out.append(_col_max8(s))
            return out
        return pass1

    def pass2(c, slot):
        for h in range(MLA_HPS):
            _fa_pass2(s_ref.at[slot, h], vT_ref[0, h, c], st, h)

    def update(mx):
        for h in range(MLA_HPS):
            _fa_update_m(st, h, mx[h])

    @pl.when(qi > 0)
    def _():
        _fa_segment(0, 2 * qi, make_pass1(False), pass2, update)

    _fa_segment(2 * qi, 2, make_pass1(True), pass2, update)
    for h in range(MLA_HPS):
        oT_ref[0, h * V_DIM:(h + 1) * V_DIM, :] = _fa_result(st, h, V_DIM)


def _mla(qmT, km, vmT5):
    B, _, S = qmT.shape
    hp = MLA_HPS
    return pl.pallas_call(
        _mla_kernel,
        grid=(B, MLA_HEADS // hp, S // TQ),
        in_specs=[pl.BlockSpec((1, hp * MLA_QK, TQ), lambda b, h, i: (b, h, i)),
                  pl.BlockSpec((1, S, hp * MLA_QK), lambda b, h, i: (b, 0, h)),
                  pl.BlockSpec((1, hp, S // TK, V_DIM + V_ONES, TK), lambda b, h, i: (b, h, 0, 0, 0))],
        out_specs=pl.BlockSpec((1, hp * V_DIM, TQ), lambda b, h, i: (b, h, i)),
        out_shape=jax.ShapeDtypeStruct((B, MLA_HEADS * V_DIM, S), F32),
        scratch_shapes=[pltpu.VMEM((2, hp, TK, TQ), F32)] + [pltpu.VMEM((hp, 8, TQ), F32)] * 2
        + [pltpu.VMEM((hp, V_DIM + V_ONES, TQ), F32)],
        compiler_params=_cparams("parallel", "parallel", "parallel"),
        name="mla",
    )(qmT, km, vmT5)


def _outproj_kernel(ocT_ref, osT_ref, owT_ref, omT_ref, gT_ref, x_ref, gn_ref, gm_ref, wout_ref,
                    g2_ref, x2_ref, hnT_ref):
    parts = []
    for h in range(NSA_HEADS):
        rows = slice(h * HEAD_DIM, (h + 1) * HEAD_DIM)
        parts.append(gT_ref[0, 3 * h:3 * h + 1, :] * ocT_ref[0, rows, :]
                     + gT_ref[0, 3 * h + 1:3 * h + 2, :] * osT_ref[0, rows, :]
                     + gT_ref[0, 3 * h + 2:3 * h + 3, :] * owT_ref[0, rows, :])
    nsaT = jnp.concatenate(parts, axis=0)

    def norm_t(yT, g):
        y = yT * lax.rsqrt(jnp.mean(yT * yT, axis=0, keepdims=True) + EPS)
        return (y.T * g).astype(BF16)

    y_nsa = norm_t(nsaT, gn_ref[...])
    y_mla = norm_t(omT_ref[0], gm_ref[...])
    x2 = x_ref[0] + _dot(y_nsa, wout_ref[:NSA_W, :]) + _dot(y_mla, wout_ref[NSA_W:, :])
    x2_ref[0] = x2
    hnT_ref[...] = (_rms(x2) * g2_ref[...]).T.astype(BF16)


def _outproj(ocT, osT, owT, omT, gT, x, gn, gm, wout, g2):
    B, S, D = x.shape
    tm = TM_OUT
    nt = S // tm
    tr = lambda a: pl.BlockSpec((1, a.shape[1], tm), lambda b, i: (b, 0, i))
    full = lambda a: pl.BlockSpec(a.shape, lambda b, i: (0,) * a.ndim)
    return pl.pallas_call(
        _outproj_kernel,
        grid=(B, nt),
        in_specs=[tr(ocT), tr(osT), tr(owT), tr(omT), tr(gT),
                  pl.BlockSpec((1, tm, D), lambda b, i: (b, i, 0)),
                  full(gn), full(gm), full(wout), full(g2)],
        out_specs=[pl.BlockSpec((1, tm, D), lambda b, i: (b, i, 0)),
                   pl.BlockSpec((D, tm), lambda b, i: (0, b * nt + i))],
        out_shape=[jax.ShapeDtypeStruct((B, S, D), F32),
                   jax.ShapeDtypeStruct((D, B * S), BF16)],
        compiler_params=_cparams("parallel", "parallel"),
        name="outproj",
    )(ocT, osT, owT, omT, gT, x, gn, gm, wout, g2)


def _row_max_bcast(w3):
    m8 = jnp.max(w3, axis=0)
    for sh in (4, 2, 1):
        m8 = jnp.maximum(m8, pltpu.roll(m8, sh, 0))
    return m8


def _top_values(s, n):
    w3 = s.reshape(s.shape[0] // 8, 8, s.shape[1])
    vals = []
    for it in range(n):
        mx = _row_max_bcast(w3)
        vals.append(mx)
        if it < n - 1:
            w3 = jnp.where(w3 >= mx[None], REMOVED, w3)
    return vals


def _pair_list(n):
    return [(i, j) for i in range(n) for j in range(n) if (i + 1) * (j + 1) <= n]


def _top_pair_sums(a, b, n):
    cands = [a[i] + b[j] for i, j in _pair_list(n)]
    vals = []
    for it in range(n):
        mx = functools.reduce(jnp.maximum, cands)
        vals.append(mx)
        if it < n - 1:
            cands = [jnp.where(c >= mx, REMOVED, c) for c in cands]
    return vals


def _peer_topk_kernel(hnT_ref, wqT_ref, k1_ref, k2_ref, c_ref, s2_ref, e1_ref, b1_ref):
    tm = hnT_ref.shape[1]
    n = PEER_TOPK + 1
    qT = _dot(wqT_ref[...], hnT_ref[...]).astype(BF16)
    dk = k1_ref.shape[1]
    sub = lax.broadcasted_iota(jnp.int32, (8, tm), 0)
    a_m = [jnp.zeros((8, tm), F32)] * n
    b_m = [jnp.zeros((8, tm), F32)] * n
    for h in range(PEER_HEADS):
        s1 = _dot(k1_ref[...], qT[2 * h * dk:(2 * h + 1) * dk])
        s2 = _dot(k2_ref[...], qT[(2 * h + 1) * dk:(2 * h + 2) * dk])
        c_ref[h] = s1
        s2_ref[h] = s2
        a = _top_values(s1, n)
        b = _top_values(s2, n)
        a_m = [jnp.where(sub == h, a[i], a_m[i]) for i in range(n)]
        b_m = [jnp.where(sub == h, b[i], b_m[i]) for i in range(n)]
    v = _top_pair_sums(a_m, b_m, n)
    z = functools.reduce(jnp.add, [jnp.exp(v[i] - v[0]) for i in range(PEER_TOPK)])
    thr = 0.5 * (v[PEER_TOPK - 1] + v[PEER_TOPK])
    for h in range(PEER_HEADS):
        c_ref[h] = thr[h:h + 1, :] - c_ref[h]
    e1_ref[...] = thr - a_m[0] - jnp.log(z)
    b1_ref[...] = b_m[0]


def _peer_topk(hnT, wqT, k1, k2):
    D, N = hnT.shape
    tm = TM_TOPK
    H = PEER_HEADS
    nk = k1.shape[0]
    full = lambda a: pl.BlockSpec(a.shape, lambda i: (0,) * a.ndim)
    return pl.pallas_call(
        _peer_topk_kernel,
        grid=(N // tm,),
        in_specs=[pl.BlockSpec((D, tm), lambda i: (0, i)), full(wqT), full(k1), full(k2)],
        out_specs=[pl.BlockSpec((H, nk, tm), lambda i: (0, 0, i)),
                   pl.BlockSpec((H, nk, tm), lambda i: (0, 0, i)),
                   pl.BlockSpec((H, tm), lambda i: (0, i)),
                   pl.BlockSpec((H, tm), lambda i: (0, i))],
        out_shape=[jax.ShapeDtypeStruct((H, nk, N), F32),
                   jax.ShapeDtypeStruct((H, nk, N), F32),
                   jax.ShapeDtypeStruct((H, N), F32),
                   jax.ShapeDtypeStruct((H, N), F32)],
        compiler_params=_cparams("parallel"),
        name="peer_topk",
    )(hnT, wqT, k1, k2)


def _peer_dense_kernel(hnT_ref, u_ref, vT_ref, c_ref, s2_ref, e1_ref, b1_ref, x2_ref, gf_ref, out_ref,
                       acc_ref, p1_ref, p2_ref, pc_ref):
    j = pl.program_id(1)
    nk = s2_ref.shape[1]
    ne = u_ref.shape[0]
    tm = hnT_ref.shape[1]

    @pl.when(j == 0)
    def _():
        acc_ref[...] = jnp.zeros(acc_ref.shape, F32)
        for h in range(PEER_HEADS):
            p1_ref[h] = jnp.exp(e1_ref[h:h + 1, :] - c_ref[h])
            p2_ref[h] = jnp.exp(s2_ref[h] - b1_ref[h:h + 1, :])
            pc_ref[h] = jnp.exp(c_ref[h] - b1_ref[h:h + 1, :])

    act = _dot(u_ref[...], hnT_ref[...])
    ws = []
    for kk in range(ne // nk):
        i1 = j * (ne // nk) + kk
        gate = jnp.zeros((nk, tm), F32)
        for h in range(PEER_HEADS):
            p2 = p2_ref[h]
            sel = p2 >= pc_ref[h, pl.ds(i1, 1), :]
            gate = gate + jnp.where(sel, p1_ref[h, pl.ds(i1, 1), :] * p2, 0.0)
        ws.append((gate * _gelu(act[kk * nk:(kk + 1) * nk])).astype(BF16))
    acc_ref[...] += _dot(vT_ref[...], jnp.concatenate(ws, axis=0))

    @pl.when(j == pl.num_programs(1) - 1)
    def _():
        y = x2_ref[...] + acc_ref[...].T
        out_ref[...] = _rms(y) * gf_ref[...]


def _peer_dense(hnT, u, vT, c, s2, e1, b1, x2, gf):
    D, N = hnT.shape
    NE = u.shape[0]
    tm, ne = TM_PEER, NE_PEER
    H, nk, _ = s2.shape
    return pl.pallas_call(
        _peer_dense_kernel,
        grid=(N // tm, NE // ne),
        in_specs=[pl.BlockSpec((D, tm), lambda i, j: (0, i)),
                  pl.BlockSpec((ne, D), lambda i, j: (j, 0)),
                  pl.BlockSpec((D, ne), lambda i, j: (0, j)),
                  pl.BlockSpec((H, nk, tm), lambda i, j: (0, 0, i)),
                  pl.BlockSpec((H, nk, tm), lambda i, j: (0, 0, i)),
                  pl.BlockSpec((H, tm), lambda i, j: (0, i)),
                  pl.BlockSpec((H, tm), lambda i, j: (0, i)),
                  pl.BlockSpec((tm, D), lambda i, j: (i, 0)),
                  pl.BlockSpec(gf.shape, lambda i, j: (0, 0))],
        out_specs=pl.BlockSpec((tm, D), lambda i, j: (i, 0)),
        out_shape=jax.ShapeDtypeStruct((N, D), F32),
        scratch_shapes=[pltpu.VMEM((D, tm), F32)] + [pltpu.VMEM((H, nk, tm), F32)] * 3,
        compiler_params=_cparams("parallel", "arbitrary"),
        name="peer_dense",
    )(hnT, u, vT, c, s2, e1, b1, x2, gf)


def _t5_bucket(dist):
    dist = jnp.maximum(dist, 0)
    max_exact = N_BUCKETS // 2
    d = jnp.maximum(dist, 1).astype(F32)
    large = max_exact + (jnp.log(d / max_exact) / math.log(MAX_DISTANCE / max_exact)
                         * (N_BUCKETS - max_exact)).astype(jnp.int32)
    return jnp.where(dist < max_exact, dist, jnp.minimum(large, N_BUCKETS - 1))


def _toeplitz(fn, rows, roff):
    nb = rows // LANE
    x = (np.arange(2 * LANE) + LANE - 1) % (2 * LANE)
    d = LANE * (nb - 1 - np.arange(nb))[:, None] + x[None, :] - (rows - 1) + roff
    seg = fn(jnp.asarray(d.reshape(-1), jnp.int32))
    H = seg.shape[0]
    seg = seg.reshape(H, nb, 2 * LANE)
    toep = jnp.tile(seg, (1, 1, LANE))[..., :LANE * (2 * LANE - 1)]
    toep = toep.reshape(H, nb, LANE, 2 * LANE - 1)[..., :LANE]
    return toep.reshape(H, rows, LANE)


def _bias_tables(t5_bias, nc):
    onehot = lambda d: (_t5_bucket(d)[:, None] == jnp.arange(N_BUCKETS)[None, :]).astype(F32)

    def rel(d):
        return jnp.einsum("nb,bh->hn", onehot(d), t5_bias.astype(F32), precision=lax.Precision.HIGHEST)

    far = t5_bias[N_BUCKETS - 1].astype(F32)[:, None]
    nk16 = LANE // CMP_STRIDE
    n16 = 2 * nc + nk16 - 1
    x = CMP_STRIDE * (n16 - 1 - np.arange(n16))[:, None] + np.arange(CMP_STRIDE)[None, :]
    d = jnp.asarray((x - CMP_STRIDE * (nc - 1) - (CMP_LEN - 1)).reshape(-1), jnp.int32)
    f16 = jnp.where(d >= 0, rel(d) * LOG2E, NEG).reshape(t5_bias.shape[1], n16, CMP_STRIDE)
    ft = jnp.concatenate([f16[:, nk16 - 1 - k:nk16 - 1 - k + 2 * nc] for k in range(nk16)], axis=-1)
    ts = _toeplitz(lambda d: jnp.where(d >= 0, (rel(d) - far) * LOG2E, NEG), SEL_ROWS, SEL_ROFF)
    tw = _toeplitz(lambda d: jnp.where((d >= 0) & (d < WINDOW), rel(d) * LOG2E, NEG), WIN_ROWS2, WIN_ROFF2)
    return ft, ts, tw


def _overlap_t(nc, n_cmp, nsel):
    cs = np.arange(nc) * CMP_STRIDE
    ce = cs + CMP_LEN - 1
    ss = np.arange(nsel) * SEL_LEN
    ov = (cs[None, :] < ss[:, None] + SEL_LEN) & (ce[None, :] >= ss[:, None]) & (np.arange(nc)[None, :] < n_cmp)
    return jnp.asarray(ov, BF16)


def kernel(x, positions, norm1_g, w_in, cmp_pos, cmp_k_w1, cmp_k_w2, cmp_v_w1, cmp_v_w2, t5_bias,
           q_norm_g, w_q_up, kv_norm_g, w_kv_up, grp_norm_nsa, grp_norm_mla, w_out, norm2_g,
           peer_wq, peer_keys1, peer_keys2, peer_u, peer_v, final_g):
    B, S, D = x.shape
    G, H = NSA_KV_HEADS, MLA_HEADS
    assert S % TQ == 0 and S % TM_IN == 0 and norm1_g.shape[0] == 1
    nc = S // CMP_STRIDE
    n_cmp = (S - CMP_LEN) // CMP_STRIDE + 1
    nsel = S // SEL_LEN
    half = QK_ROPE // 2

    w = w_in[0]
    o_g = NSA_W + 6 * KV_W
    n_gate = 3 * NSA_HEADS
    o_cq = o_g + n_gate
    q_lora = w_q_up.shape[1]
    kv_lora = w_kv_up.shape[1]
    o_ckv = o_cq + q_lora
    o_kr = o_ckv + kv_lora
    kr1, kr2 = w[:, o_kr:o_kr + half], w[:, o_kr + half:o_kr + 2 * half]
    misc = jnp.concatenate([kr1, kr2, -kr2, kr1, w[:, o_g:o_g + n_gate],
                            jnp.zeros((D, LANE - 4 * half - n_gate), F32)], axis=1)
    win_p = jnp.concatenate([w[:, :o_g], w[:, o_cq:o_kr], misc], axis=1).astype(BF16)

    wq = w_q_up[0].reshape(q_lora, H, QK_NOPE + QK_ROPE)
    q1, q2 = wq[..., QK_NOPE:QK_NOPE + half], wq[..., QK_NOPE + half:]
    zpad = jnp.zeros((q_lora, H, MLA_QK - QK_NOPE - QK_ROPE), F32)
    wqa = jnp.concatenate([wq[..., :QK_NOPE], q1, q2, zpad], -1).reshape(q_lora, H * MLA_QK).astype(BF16)
    wqb = jnp.concatenate([jnp.zeros_like(wq[..., :QK_NOPE]), -q2, q1, zpad], -1).reshape(
        q_lora, H * MLA_QK).astype(BF16)
    wkv = w_kv_up[0].reshape(kv_lora, H, QK_NOPE + V_DIM)
    wk = jnp.concatenate([wkv[..., :QK_NOPE], jnp.zeros((kv_lora, H, MLA_QK - QK_NOPE), F32)],
                         -1).reshape(kv_lora, H * MLA_QK).astype(BF16)
    wv = wkv[..., QK_NOPE:].reshape(kv_lora, H * V_DIM).astype(BF16)
    pr = np.zeros((LANE, H, MLA_QK), np.float32)
    for j in range(QK_ROPE):
        pr[j, :, QK_NOPE + j] = 1.0
        pr[QK_ROPE + j, :, QK_NOPE + j] = 1.0
    prope = jnp.asarray(pr.reshape(LANE, H * MLA_QK), BF16)

    inv = ROPE_THETA ** (-jnp.arange(half, dtype=F32) / half)
    pos = positions.astype(F32)[..., None]
    lane = np.arange(LANE)
    ang_q = pos * jnp.concatenate([jnp.zeros((QK_NOPE,), F32), inv, inv,
                                   jnp.zeros((MLA_QK - QK_NOPE - QK_ROPE,), F32)])
    mla_scale = (QK_NOPE + QK_ROPE) ** -0.5 * LOG2E
    cq_tab = jnp.cos(ang_q) * jnp.asarray(np.where(lane < QK_NOPE + QK_ROPE, mla_scale, 0.0), F32)
    sq_tab = jnp.sin(ang_q) * mla_scale
    ang_k = pos * jnp.concatenate([inv, inv, inv, inv, jnp.zeros((LANE - 4 * half,), F32)])
    ck_tab = jnp.where(lane < 2 * half, jnp.cos(ang_k), jnp.where(lane < 4 * half, jnp.sin(ang_k), 0.0))

    row = lambda v: v.reshape(1, -1).astype(F32)
    (qT, kc, vc, ks, kw, vsT, vwT, gmT, qmT, km, vmT) = _inproj(
        x, row(norm1_g[0]), win_p, row(q_norm_g[0]), row(kv_norm_g[0]), wqa, wqb, wk, wv, prope,
        cq_tab, sq_tab, ck_tab)

    def chunks16(a):
        return a.reshape(B, nc, CMP_STRIDE, G, HEAD_DIM).transpose(0, 3, 1, 2, 4).reshape(
            B * G, nc, CMP_STRIDE * HEAD_DIM)

    def heads_first(a):
        return a.reshape(B, S, G, HEAD_DIM).transpose(0, 2, 1, 3)

    def lane_chunks(aT, nh, dh, ck):
        L = aT.shape[-1]
        return aT.reshape(B, nh, dh, L // ck, ck).transpose(0, 1, 3, 2, 4)

    c_all = jnp.stack([chunks16(kc), chunks16(vc)])
    pos_lo = cmp_pos[0, :CMP_STRIDE].reshape(1, -1)
    pos_hi = cmp_pos[0, CMP_STRIDE:].reshape(1, -1)
    split = CMP_STRIDE * HEAD_DIM
    w1 = jnp.stack([cmp_k_w1[0], cmp_v_w1[0]]).astype(BF16)
    w2 = jnp.stack([cmp_k_w2[0], cmp_v_w2[0]]).astype(BF16)
    cmp_out = _compress(c_all, pos_lo, pos_hi, w1[:, :split], w1[:, split:], w2)
    kcmp = cmp_out[0]
    vcmpT = cmp_out[1].transpose(0, 2, 1)

    ft, ts, tw = _bias_tables(t5_bias, nc)
    ocT, selb = _nsa_cmp(qT, kcmp, vcmpT, ft, _overlap_t(nc, n_cmp, nsel))
    osT, owT = _nsa_sel(qT, heads_first(ks), _with_ones_rows(lane_chunks(vsT, G, HEAD_DIM, TK)), selb, ts,
                        heads_first(kw), _with_ones_rows(lane_chunks(vwT, G, HEAD_DIM, TK)), tw)
    omT = _mla(qmT, km, _with_ones_rows(lane_chunks(vmT, H, V_DIM, TK)))

    o_gate = 4 * half
    x2, hnT = _outproj(ocT, osT, owT, omT, gmT[:, o_gate:o_gate + n_gate], x, row(grp_norm_nsa[0]),
                       row(grp_norm_mla[0]), w_out[0].astype(BF16), row(norm2_g[0]))

    c, s2, e1, b1 = _peer_topk(hnT, peer_wq[0].T.astype(BF16), peer_keys1[0].astype(BF16),
                               peer_keys2[0].astype(BF16))
    out = _peer_dense(hnT, peer_u[0].astype(BF16), peer_v[0].T.astype(BF16), c, s2, e1, b1,
                      x2.reshape(B * S, D), row(final_g))
    return out.reshape(B, S, D)
```
